```python
import math
import jax
import jax.numpy as jnp
from jax import lax
import numpy as np

D_MODEL = 1024
BATCH = 8
SEQ = 4096
DEPTH = 2

D_PLE = 256
CONV_WIDTH = 4
NORM_EPS = 1e-6
RG_WIDTH = D_MODEL // 2
RG_BLOCKS = 8
RG_BLOCK = RG_WIDTH // RG_BLOCKS
RG_C = 8.0
GDN_HEADS = 8
GDN_DK = 128
GDN_DV = 128
GDN_KEY = GDN_HEADS * GDN_DK
GDN_VAL = GDN_HEADS * GDN_DV
GDN_CHUNK = 64
S5_WIDTH = D_MODEL // 2
S5_GROUP = 16
S5_GROUPS = S5_WIDTH // S5_GROUP
S5_STATE = 64
D_MIX = RG_WIDTH + GDN_VAL + S5_WIDTH
N_IN = 2 * RG_WIDTH + 2 * GDN_KEY + 2 * GDN_VAL + 2 * GDN_HEADS + 2 * S5_WIDTH

kernel_name = 'hymba_style_rglru_gdn_s5_hybrid'


def rms_norm(x, g):
    xf = x.astype(jnp.float32)
    y = xf * lax.rsqrt(jnp.mean(xf * xf, axis=-1, keepdims=True) + NORM_EPS)
    return (y * g.astype(jnp.float32)).astype(x.dtype)


def l2_normalize(x):
    return x * lax.rsqrt(jnp.sum(x * x, axis=-1, keepdims=True) + NORM_EPS)


def causal_dwconv(x, w, b=None):
    k_w, ch = w.shape
    y = lax.conv_general_dilated(x, w[:, None, :].astype(x.dtype), window_strides=(1,),
                                 padding=((k_w - 1, 0),), dimension_numbers=('NWC', 'WIO', 'NWC'),
                                 feature_group_count=ch)
    if b is not None:
        y = y + b.astype(y.dtype)
    return y


def linear_combine(left, right):
    a1, b1 = left
    a2, b2 = right
    return a1 * a2, a2 * b1 + b2


def split_columns(proj):
    sizes = [RG_WIDTH, RG_WIDTH, GDN_KEY, GDN_KEY, GDN_VAL, GDN_VAL, GDN_HEADS, GDN_HEADS, S5_WIDTH, S5_WIDTH]
    offsets = [int(o) for o in np.cumsum(sizes)[:-1]]
    return jnp.split(proj, offsets, axis=-1)


def rg_lru(x, w_a, b_a, w_x, b_x, lam):
    f32 = jnp.float32
    xf = x.astype(f32)
    b_, s_, _ = x.shape
    xb = xf.reshape(b_, s_, RG_BLOCKS, RG_BLOCK)
    r = jax.nn.sigmoid(jnp.einsum('bshi,hij->bshj', xb, w_a.astype(f32)).reshape(b_, s_, RG_WIDTH) + b_a.astype(f32))
    gi = jax.nn.sigmoid(jnp.einsum('bshi,hij->bshj', xb, w_x.astype(f32)).reshape(b_, s_, RG_WIDTH) + b_x.astype(f32))
    log_a = -RG_C * r * jax.nn.softplus(-lam.astype(f32))
    a = jnp.exp(log_a)
    inp = jnp.sqrt(-jnp.expm1(2.0 * log_a)) * (gi * xf)
    _, h = lax.associative_scan(linear_combine, (a, inp), axis=1)
    return h


def gated_delta_rule(q, k, v, g, beta):
    b_, s_, nh, dk = q.shape
    dv = v.shape[-1]
    c = GDN_CHUNK
    n = s_ // c

    def chunk(t):
        t = t.reshape((b_, n, c, nh) + t.shape[3:])
        return jnp.moveaxis(t, (1, 3), (0, 2))

    qc, kc, vc, gc, bc = chunk(q), chunk(k), chunk(v), chunk(g), chunk(beta)
    gcum = jnp.cumsum(gc, axis=-1)
    causal = jnp.tril(jnp.ones((c, c), dtype=bool))
    strict = jnp.tril(jnp.ones((c, c), dtype=bool), -1)
    diff = gcum[..., :, None] - gcum[..., None, :]
    decay = jnp.where(causal, jnp.exp(jnp.where(causal, diff, 0.0)), 0.0)
    kb = kc * bc[..., None]
    a_kk = jnp.where(strict, jnp.einsum('nbhid,nbhjd->nbhij', kb, kc) * decay, 0.0)
    lhs = a_kk + jnp.eye(c, dtype=jnp.float32)
    rhs = jnp.concatenate([vc * bc[..., None], kb * jnp.exp(gcum)[..., None]], axis=-1)
    sol = lax.linalg.triangular_solve(lhs, rhs, left_side=True, lower=True, unit_diagonal=True)
    u_c, w_c = sol[..., :dv], sol[..., dv:]
    a_qk = jnp.einsum('nbhid,nbhjd->nbhij', qc, kc) * decay

    def step(state, inp):
        q_i, k_i, u_i, w_i, aqk_i, gc_i = inp
        v_new = u_i - jnp.einsum('bhcd,bhde->bhce', w_i, state)
        o_i = jnp.einsum('bhcd,bhde->bhce', q_i * jnp.exp(gc_i)[..., None], state) + jnp.einsum('bhij,bhje->bhie', aqk_i, v_new)
        g_last = gc_i[..., -1]
        k_dec = k_i * jnp.exp(g_last[..., None] - gc_i)[..., None]
        state = state * jnp.exp(g_last)[..., None, None] + jnp.einsum('bhcd,bhce->bhde', k_dec, v_new)
        return state, o_i

    state0 = jnp.zeros((b_, nh, dk, dv), jnp.float32)
    _, o = lax.scan(step, state0, (qc, kc, u_c, w_c, a_qk, gcum))
    return jnp.moveaxis(o, (0, 2), (1, 3)).reshape(b_, s_, nh, dv)


def gdn_branch(q, k, v, z, b, a, conv_w, a_log, dt_bias, norm_g):
    f32 = jnp.float32
    b_, s_, _ = q.shape
    qkv = jax.nn.silu(causal_dwconv(jnp.concatenate([q, k, v], axis=-1), conv_w)).astype(f32)
    q, k, v = jnp.split(qkv, [GDN_KEY, 2 * GDN_KEY], axis=-1)
    q = l2_normalize(q.reshape(b_, s_, GDN_HEADS, GDN_DK)) * (GDN_DK ** -0.5)
    k = l2_normalize(k.reshape(b_, s_, GDN_HEADS, GDN_DK))
    v = v.reshape(b_, s_, GDN_HEADS, GDN_DV)
    beta = jax.nn.sigmoid(b.astype(f32))
    g = -jnp.exp(a_log.astype(f32)) * jax.nn.softplus(a.astype(f32) + dt_bias.astype(f32))
    o = gated_delta_rule(q, k, v, g, beta)
    o = rms_norm(o, norm_g) * jax.nn.silu(z.astype(f32).reshape(b_, s_, GDN_HEADS, GDN_DV))
    return o.reshape(b_, s_, GDN_VAL)


def s5_ssm(u, a_re, a_im, b_re, b_im, c_re, c_im, d, log_dt):
    f32 = jnp.float32
    lam = lax.complex(a_re.astype(f32), a_im.astype(f32))
    dt = jnp.exp(log_dt.astype(f32))[:, None]
    lam_bar = jnp.exp(lam * dt)
    b_bar = ((lam_bar - 1.0) / lam)[..., None] * lax.complex(b_re.astype(f32), b_im.astype(f32))
    c_mat = lax.complex(c_re.astype(f32), c_im.astype(f32))
    d_g = d.astype(f32).reshape(S5_GROUPS, S5_GROUP)

    def one(u_b):
        ub = u_b.reshape(u_b.shape[0], S5_GROUPS, S5_GROUP)
        bu = jnp.einsum('gnc,sgc->sgn', b_bar, ub.astype(jnp.complex64))
        a = jnp.broadcast_to(lam_bar, bu.shape)
        _, xs = lax.associative_scan(linear_combine, (a, bu), axis=0)
        y = jnp.real(jnp.einsum('gcn,sgn->sgc', c_mat, xs)) + d_g * ub
        return y.reshape(u_b.shape)

    return lax.map(one, u.astype(f32))


def setup_inputs(seed: int = 0) -> dict:
    key = jax.random.key(seed)
    ks = jax.random.split(key, 32)
    f32 = jnp.float32
    nrm = lambda k, shape, s: jax.random.normal(k, shape, f32) * s
    L = DEPTH
    u_rg = jax.random.uniform(ks[10], (L, RG_WIDTH), f32, 0.9, 0.999)
    a_rg = u_rg ** (1.0 / RG_C)
    dt_gdn = jnp.exp(jax.random.uniform(ks[13], (L, GDN_HEADS), f32, math.log(1e-3), math.log(1e-1)))
    n_idx = jnp.arange(S5_STATE, dtype=f32)
    return {
        'x': nrm(ks[0], (BATCH, SEQ, D_MODEL), 1.0),
        'p': nrm(ks[1], (DEPTH, BATCH, SEQ, D_PLE), 1.0),
        'norm_g': 1.0 + nrm(ks[2], (L, D_MODEL), 0.02),
        'w_in': nrm(ks[3], (L, D_MODEL, N_IN), D_MODEL ** -0.5),
        'rg_conv_w': nrm(ks[4], (L, CONV_WIDTH, RG_WIDTH), CONV_WIDTH ** -0.5),
        'rg_conv_b': nrm(ks[5], (L, RG_WIDTH), 0.01),
        'rg_w_a': nrm(ks[6], (L, RG_BLOCKS, RG_BLOCK, RG_BLOCK), RG_BLOCK ** -0.5),
        'rg_b_a': nrm(ks[7], (L, RG_WIDTH), 0.01),
        'rg_w_x': nrm(ks[8], (L, RG_BLOCKS, RG_BLOCK, RG_BLOCK), RG_BLOCK ** -0.5),
        'rg_b_x': nrm(ks[9], (L, RG_WIDTH), 0.01),
        'rg_lambda': jnp.log(a_rg) - jnp.log1p(-a_rg),
        'gdn_conv_w': nrm(ks[11], (L, CONV_WIDTH, 2 * GDN_KEY + GDN_VAL), CONV_WIDTH ** -0.5),
        'gdn_a_log': jnp.log(jax.random.uniform(ks[12], (L, GDN_HEADS), f32, 1.0, 16.0)),
        'gdn_dt_bias': dt_gdn + jnp.log(-jnp.expm1(-dt_gdn)),
        'gdn_norm_g': 1.0 + nrm(ks[14], (L, GDN_DV), 0.02),
        's5_a_re': -0.5 + nrm(ks[15], (L, S5_GROUPS, S5_STATE), 0.01),
        's5_a_im': jnp.pi * n_idx + nrm(ks[16], (L, S5_GROUPS, S5_STATE), 0.01),
        's5_b_re': nrm(ks[17], (L, S5_GROUPS, S5_STATE, S5_GROUP), (2.0 * S5_GROUP) ** -0.5),
        's5_b_im': nrm(ks[18], (L, S5_GROUPS, S5_STATE, S5_GROUP), (2.0 * S5_GROUP) ** -0.5),
        's5_c_re': nrm(ks[19], (L, S5_GROUPS, S5_GROUP, S5_STATE), (2.0 * S5_STATE) ** -0.5),
        's5_c_im': nrm(ks[20], (L, S5_GROUPS, S5_GROUP, S5_STATE), (2.0 * S5_STATE) ** -0.5),
        's5_d': nrm(ks[21], (L, S5_WIDTH), 1.0),
        's5_log_dt': jax.random.uniform(ks[22], (L, S5_GROUPS), f32, math.log(1e-3), math.log(1e-1)),
        's5_w_glu': nrm(ks[23], (L, S5_WIDTH, S5_WIDTH), S5_WIDTH ** -0.5),
        's5_b_glu': nrm(ks[24], (L, S5_WIDTH), 0.01),
        'w_out': nrm(ks[25], (L, D_MIX, D_MODEL), D_MIX ** -0.5),
        'ple_norm_g': 1.0 + nrm(ks[26], (L, D_MODEL), 0.02),
        'ple_w_gate': nrm(ks[27], (L, D_MODEL, D_MODEL), D_MODEL ** -0.5),
        'ple_w_proj': nrm(ks[28], (L, D_PLE, D_MODEL), D_PLE ** -0.5),
        'final_norm_g': 1.0 + nrm(ks[29], (D_MODEL,), 0.02),
    }


def reference(x, p, norm_g, w_in, rg_conv_w, rg_conv_b, rg_w_a, rg_b_a, rg_w_x, rg_b_x, rg_lambda,
              gdn_conv_w, gdn_a_log, gdn_dt_bias, gdn_norm_g, s5_a_re, s5_a_im, s5_b_re, s5_b_im,
              s5_c_re, s5_c_im, s5_d, s5_log_dt, s5_w_glu, s5_b_glu, w_out, ple_norm_g, ple_w_gate,
              ple_w_proj, final_norm_g):
    dt = x.dtype
    f32 = jnp.float32
    h = x
    for i in range(DEPTH):
        hn = rms_norm(h, norm_g[i])
        proj = hn @ w_in[i]
        rg_x, rg_gate, gq, gk, gv, gz, gb, ga, s5_u, s5_gate = split_columns(proj)
        xr = causal_dwconv(rg_x, rg_conv_w[i], rg_conv_b[i])
        y_rg = rg_lru(xr, rg_w_a[i], rg_b_a[i], rg_w_x[i], rg_b_x[i], rg_lambda[i]) * jax.nn.silu(rg_gate.astype(f32))
        y_gdn = gdn_branch(gq, gk, gv, gz, gb, ga, gdn_conv_w[i], gdn_a_log[i], gdn_dt_bias[i], gdn_norm_g[i])
        y_s5 = s5_ssm(s5_u, s5_a_re[i], s5_a_im[i], s5_b_re[i], s5_b_im[i], s5_c_re[i], s5_c_im[i], s5_d[i], s5_log_dt[i])
        z5 = jax.nn.gelu(y_s5)
        y_s5 = z5 * jax.nn.sigmoid(z5 @ s5_w_glu[i].astype(f32) + s5_b_glu[i].astype(f32))
        y_s5 = y_s5 * jax.nn.silu(s5_gate.astype(f32))
        mix = jnp.concatenate([y_rg.astype(dt), y_gdn.astype(dt), y_s5.astype(dt)], axis=-1)
        h = h + mix @ w_out[i]
        gate = jax.nn.sigmoid(rms_norm(h, ple_norm_g[i]) @ ple_w_gate[i])
        h = h + gate * (p[i] @ ple_w_proj[i])
    return rms_norm(h, final_norm_g)
```

```python
import functools

import jax
import jax.numpy as jnp
from jax import lax
from jax.experimental import pallas as pl
from jax.experimental.pallas import tpu as pltpu

F32 = jnp.float32
BF16 = jnp.bfloat16

NORM_EPS = 1e-6
RG_C = 8.0
CONV_K = 4

SUBLANES = 8
LANES = 128
TILE_T = 64
ROWS = TILE_T * SUBLANES
HALO = (CONV_K - 1) * SUBLANES

RG_W = 512
S5_W = 512
S5_GROUP = 16
S5_STATE = 64
S5_LANE_GROUPS = LANES // S5_GROUP
S5_BLK_STATE = S5_LANE_GROUPS * S5_STATE
GDN_H = 8
GDN_D = 128
GDN_W = GDN_H * GDN_D

VMEM_LIMIT = 56 * 1024 * 1024


def _rms(x, g):
    ms = jnp.mean(x * x, axis=-1, keepdims=True)
    return x * lax.rsqrt(ms + NORM_EPS) * g


def _sigmoid(x):
    return 1.0 / (1.0 + jnp.exp(-x))


def _silu(x):
    return x * _sigmoid(x)


def _softplus(x):
    return jnp.maximum(x, 0.0) + jnp.log1p(jnp.exp(-jnp.abs(x)))


def _expm1(x):
    u = jnp.exp(x)
    um1 = u - 1.0
    regular = jnp.logical_and(um1 != 0.0, um1 != -1.0)
    v = um1 * x / jnp.log(jnp.where(regular, u, 2.0))
    return jnp.where(um1 == 0.0, x, jnp.where(um1 == -1.0, -1.0, v))


def _dot(a, b):
    return jnp.dot(a.astype(BF16), b.astype(BF16), preferred_element_type=F32)


def _dot_nt(a, b):
    return lax.dot_general(a.astype(BF16), b.astype(BF16), (((1,), (1,)), ((), ())),
                           preferred_element_type=F32)


def _dot_tn(a, b):
    return lax.dot_general(a.astype(BF16), b.astype(BF16), (((0,), (0,)), ((), ())),
                           preferred_element_type=F32)


def _causal_conv(x, buf, cw):
    buf[HALO:HALO + ROWS, :] = x
    y = cw[CONV_K - 1:CONV_K, :] * x
    for k in range(CONV_K - 1):
        off = HALO - (CONV_K - 1 - k) * SUBLANES
        y = y + cw[k:k + 1, :] * buf[off:off + ROWS, :]
    buf[0:HALO, :] = buf[ROWS:ROWS + HALO, :]
    return y


def _rg_s5_kernel(h_ref, ng_ref, w_ref, cw_ref, cb_ref, wg_ref, ba_ref, bx_ref, c_ref,
                  bm_ref, lr_ref, li_ref, cm_ref, d_ref, wglu_ref, bglu_ref,
                  y_ref,
                  xbuf, a_buf, i_buf, hcar, st_buf, scar):
    @pl.when(pl.program_id(0) == 0)
    def _():
        xbuf[0:HALO, :] = jnp.zeros((HALO, RG_W), F32)
        hcar[...] = jnp.zeros_like(hcar)
        scar[...] = jnp.zeros_like(scar)

    hn = _rms(h_ref[...], ng_ref[...]).astype(BF16)
    proj = jnp.dot(hn, w_ref[...], preferred_element_type=F32)
    rg_x = proj[:, 0:RG_W]
    rg_gate = proj[:, RG_W:2 * RG_W]
    s5_u = proj[:, 2 * RG_W:2 * RG_W + S5_W]
    s5_gate = proj[:, 2 * RG_W + S5_W:]

    xr = _causal_conv(rg_x, xbuf, cw_ref[...]) + cb_ref[...]
    pre_a, pre_x = [], []
    for j in range(RG_W // LANES):
        g = _dot(xr[:, j * LANES:(j + 1) * LANES], wg_ref[j])
        pre_a.append(g[:, :LANES])
        pre_x.append(g[:, LANES:])
    r = _sigmoid(jnp.concatenate(pre_a, axis=1) + ba_ref[...])
    gi = _sigmoid(jnp.concatenate(pre_x, axis=1) + bx_ref[...])
    log_a = c_ref[...] * r
    a_buf[...] = jnp.exp(log_a)
    i_buf[...] = jnp.sqrt(-_expm1(2.0 * log_a)) * (gi * xr)

    def rg_step(t, hc):
        r0 = pl.multiple_of(t * SUBLANES, SUBLANES)
        hc = a_buf[pl.ds(r0, SUBLANES), :] * hc + i_buf[pl.ds(r0, SUBLANES), :]
        i_buf[pl.ds(r0, SUBLANES), :] = hc
        return hc

    hcar[...] = lax.fori_loop(0, TILE_T, rg_step, hcar[...], unroll=8)
    y_ref[:, 0:RG_W] = (i_buf[...] * _silu(rg_gate)).astype(y_ref.dtype)

    nblk = S5_W // LANES
    sw = 2 * S5_BLK_STATE
    for j in range(nblk):
        st_buf[:, j * sw:(j + 1) * sw] = _dot(s5_u[:, j * LANES:(j + 1) * LANES], bm_ref[j])
    for j in range(nblk):
        c_re = j * sw
        c_im = j * sw + S5_BLK_STATE
        lr = jnp.broadcast_to(lr_ref[:, j * S5_BLK_STATE:(j + 1) * S5_BLK_STATE], (SUBLANES, S5_BLK_STATE))
        li = jnp.broadcast_to(li_ref[:, j * S5_BLK_STATE:(j + 1) * S5_BLK_STATE], (SUBLANES, S5_BLK_STATE))

        def s5_step(t, carry, c_re=c_re, c_im=c_im, lr=lr, li=li):
            sr, si = carry
            r0 = pl.multiple_of(t * SUBLANES, SUBLANES)
            nr = lr * sr - li * si + st_buf[pl.ds(r0, SUBLANES), c_re:c_re + S5_BLK_STATE]
            ni = lr * si + li * sr + st_buf[pl.ds(r0, SUBLANES), c_im:c_im + S5_BLK_STATE]
            st_buf[pl.ds(r0, SUBLANES), c_re:c_re + S5_BLK_STATE] = nr
            st_buf[pl.ds(r0, SUBLANES), c_im:c_im + S5_BLK_STATE] = ni
            return nr, ni

        sr, si = lax.fori_loop(
            0, TILE_T, s5_step,
            (scar[:, c_re:c_re + S5_BLK_STATE], scar[:, c_im:c_im + S5_BLK_STATE]), unroll=4)
        scar[:, c_re:c_re + S5_BLK_STATE] = sr
        scar[:, c_im:c_im + S5_BLK_STATE] = si
    ys = [_dot(st_buf[:, j * sw:(j + 1) * sw], cm_ref[j]) for j in range(nblk)]
    y5 = jnp.concatenate(ys, axis=1) + d_ref[...] * s5_u
    z5 = jax.nn.gelu(y5)
    y5 = z5 * _sigmoid(_dot(z5, wglu_ref[...]) + bglu_ref[...])
    y_ref[:, RG_W:RG_W + S5_W] = (y5 * _silu(s5_gate)).astype(y_ref.dtype)


def _gdn_kernel(h_ref, ng_ref, wqkv_ref, wz_ref, wab_ref, cw_ref, nea_ref, dtb_ref, gn_ref,
                y_ref,
                cbuf, hn_buf, k_s, kb_s, q_s, qe_s, vb_s, kbe_s, kd_s, gc_s, o_s, st_ref):
    @pl.when(pl.program_id(0) == 0)
    def _():
        cbuf[0:HALO, :] = jnp.zeros((HALO, 3 * GDN_W), F32)
        st_ref[...] = jnp.zeros_like(st_ref)

    hn_buf[...] = _rms(h_ref[...], ng_ref[...]).astype(BF16)

    ab = jnp.dot(hn_buf[...], wab_ref[...], preferred_element_type=F32)
    g = nea_ref[...] * _softplus(ab[:, :LANES] + dtb_ref[...])
    beta = _sigmoid(ab[:, LANES:])
    gc = g
    d = 1
    while d < TILE_T:
        sh = d * SUBLANES
        gc = gc + jnp.concatenate([jnp.zeros((sh, LANES), F32), gc[:ROWS - sh, :]], axis=0)
        d *= 2
    g_last = jnp.broadcast_to(gc[ROWS - SUBLANES:, :][None], (TILE_T, SUBLANES, LANES)).reshape(ROWS, LANES)
    egc = jnp.exp(gc)
    edec = jnp.exp(g_last - gc)
    gc_s[...] = gc

    def conv_act(c):
        pre = jnp.dot(hn_buf[...], wqkv_ref[:, c * GDN_W:(c + 1) * GDN_W], preferred_element_type=F32)
        buf = cbuf.at[:, c * GDN_W:(c + 1) * GDN_W]
        return _silu(_causal_conv(pre, buf, cw_ref[:, c * GDN_W:(c + 1) * GDN_W]))

    q_act = conv_act(0)
    for h in range(GDN_H):
        sl = slice(h * GDN_D, (h + 1) * GDN_D)
        qh = q_act[:, sl]
        qn = qh * (lax.rsqrt(jnp.sum(qh * qh, axis=-1, keepdims=True) + NORM_EPS) * (GDN_D ** -0.5))
        q_s[h] = qn
        qe_s[h] = qn * egc[:, h:h + 1]
    k_act = conv_act(1)
    for h in range(GDN_H):
        sl = slice(h * GDN_D, (h + 1) * GDN_D)
        kh = k_act[:, sl]
        kn = kh * lax.rsqrt(jnp.sum(kh * kh, axis=-1, keepdims=True) + NORM_EPS)
        kb = kn * beta[:, h:h + 1]
        k_s[h] = kn
        kb_s[h] = kb
        kbe_s[h] = kb * egc[:, h:h + 1]
        kd_s[h] = kn * edec[:, h:h + 1]
    v_act = conv_act(2)
    for h in range(GDN_H):
        sl = slice(h * GDN_D, (h + 1) * GDN_D)
        vb_s[h] = v_act[:, sl] * beta[:, h:h + 1]

    row = lax.broadcasted_iota(jnp.int32, (TILE_T, TILE_T), 0)
    col = lax.broadcasted_iota(jnp.int32, (TILE_T, TILE_T), 1)
    causal = row >= col
    strict = row > col
    eye = (row == col).astype(F32)

    def per_batch(b, carry):
        rows = pl.ds(b, TILE_T, stride=SUBLANES)
        gb = gc_s[rows, :]
        gt = jnp.concatenate([gb, jnp.zeros_like(gb)], axis=0).T
        for h in range(GDN_H):
            k_b = k_s[h, rows, :]
            diff = gb[:, h:h + 1] - gt[h:h + 1, 0:TILE_T]
            dec = jnp.where(causal, jnp.exp(jnp.where(causal, diff, 0.0)), 0.0)
            kk = _dot_nt(jnp.concatenate([kb_s[h, rows, :], q_s[h, rows, :]], axis=0), k_b)
            a_kk = jnp.where(strict, kk[:TILE_T] * dec, 0.0)
            a_qk = kk[TILE_T:] * dec
            neg = -a_kk
            tinv = eye + neg
            pw = _dot(neg, neg)
            span = 2
            while span < TILE_T:
                tinv = tinv + _dot(tinv, pw)
                span *= 2
                if span < TILE_T:
                    pw = _dot(pw, pw)
            uw = _dot(tinv, jnp.concatenate([vb_s[h, rows, :], kbe_s[h, rows, :]], axis=1))
            u = uw[:, :GDN_D]
            w = uw[:, GDN_D:]
            state = st_ref[b * GDN_H + h]
            ws_qs = _dot(jnp.concatenate([w, qe_s[h, rows, :]], axis=0), state)
            v_new = u - ws_qs[:TILE_T]
            o_s[h, rows, :] = ws_qs[TILE_T:] + _dot(a_qk, v_new)
            e_last = jnp.exp(gt[h:h + 1, TILE_T - 1:TILE_T])
            st_ref[b * GDN_H + h] = state * e_last + _dot_tn(kd_s[h, rows, :], v_new)
        return carry

    lax.fori_loop(0, SUBLANES, per_batch, 0)

    z = jnp.dot(hn_buf[...], wz_ref[...], preferred_element_type=F32)
    for h in range(GDN_H):
        sl = slice(h * GDN_D, (h + 1) * GDN_D)
        y_ref[:, sl] = (_rms(o_s[h], gn_ref[...]) * _silu(z[:, sl])).astype(y_ref.dtype)


def _out_kernel(h_ref, ya_ref, yg_ref, p_ref, woa_ref, wog_ref, png_ref, wgate_ref, wproj_ref, fg_ref,
                o_ref, *, final):
    h = h_ref[...] + jnp.dot(ya_ref[...], woa_ref[...], preferred_element_type=F32) \
        + jnp.dot(yg_ref[...], wog_ref[...], preferred_element_type=F32)
    gate = _sigmoid(_dot(_rms(h, png_ref[...]), wgate_ref[...]))
    h = h + gate * _dot(p_ref[...], wproj_ref[...])
    o_ref[...] = _rms(h, fg_ref[...]) if final else h


def _full(shape):
    nd = len(shape)
    return pl.BlockSpec(shape, lambda i: (0,) * nd)


def _rows(width):
    return pl.BlockSpec((ROWS, width), lambda i: (i, 0))


def _seq_params():
    return pltpu.CompilerParams(dimension_semantics=("arbitrary",), vmem_limit_bytes=VMEM_LIMIT)


def _block_diag(blocks):
    n, r, c = blocks.shape
    eye = jnp.eye(n, dtype=blocks.dtype)
    return jnp.einsum('grc,gh->grhc', blocks, eye).reshape(n * r, n * c)


def _s5_operators(a_re, a_im, b_re, b_im, c_re, c_im, log_dt):
    lam = lax.complex(a_re, a_im)
    dt = jnp.exp(log_dt)[:, None]
    lam_bar = jnp.exp(lam * dt)
    b_bar = ((lam_bar - 1.0) / lam)[..., None] * lax.complex(b_re, b_im)
    nblk = S5_W // LANES
    bt = jnp.swapaxes(b_bar, 1, 2).reshape(nblk, S5_LANE_GROUPS, S5_GROUP, S5_STATE)
    bm = jnp.concatenate([jax.vmap(_block_diag)(jnp.real(bt)), jax.vmap(_block_diag)(jnp.imag(bt))], axis=-1)
    ct = jnp.swapaxes(lax.complex(c_re, c_im), 1, 2).reshape(nblk, S5_LANE_GROUPS, S5_STATE, S5_GROUP)
    cm = jnp.concatenate([jax.vmap(_block_diag)(jnp.real(ct)), -jax.vmap(_block_diag)(jnp.imag(ct))], axis=1)
    return (bm.astype(BF16), jnp.real(lam_bar).reshape(1, -1), jnp.imag(lam_bar).reshape(1, -1),
            cm.astype(BF16))


def _layer(h, p_i, prm, final, final_g):
    n_rows = h.shape[0]
    grid = (n_rows // ROWS,)
    d_model = h.shape[1]
    (norm_g, w_in, rg_conv_w, rg_conv_b, rg_w_a, rg_b_a, rg_w_x, rg_b_x, rg_lambda,
     gdn_conv_w, gdn_a_log, gdn_dt_bias, gdn_norm_g, s5_a_re, s5_a_im, s5_b_re, s5_b_im,
     s5_c_re, s5_c_im, s5_d, s5_log_dt, s5_w_glu, s5_b_glu, w_out, ple_norm_g, ple_w_gate,
     ple_w_proj) = prm

    o_rgx, o_rgg = 0, RG_W
    o_q = 2 * RG_W
    o_z = o_q + 3 * GDN_W
    o_b = o_z + GDN_W
    o_a = o_b + GDN_H
    o_u = o_a + GDN_H
    o_sg = o_u + S5_W
    row1 = lambda v: v.reshape(1, -1).astype(F32)

    w_a = jnp.concatenate([w_in[:, o_rgx:o_rgx + 2 * RG_W], w_in[:, o_u:o_u + 2 * S5_W]], axis=1).astype(BF16)
    nb = RG_W // LANES
    per = rg_w_a.shape[0] // nb
    wg = jnp.stack([jnp.concatenate([_block_diag(rg_w_a[j * per:(j + 1) * per]),
                                     _block_diag(rg_w_x[j * per:(j + 1) * per])], axis=1)
                    for j in range(nb)]).astype(BF16)
    c_rg = row1(-RG_C * jax.nn.softplus(-rg_lambda))
    bm, lam_re, lam_im, cm = _s5_operators(s5_a_re, s5_a_im, s5_b_re, s5_b_im, s5_c_re, s5_c_im, s5_log_dt)
    n_state = 2 * S5_BLK_STATE * (S5_W // LANES)
    y_a = pl.pallas_call(
        _rg_s5_kernel,
        grid=grid,
        in_specs=[_rows(d_model), _full((1, d_model)), _full(w_a.shape), _full((CONV_K, RG_W)),
                  _full((1, RG_W)), _full(wg.shape), _full((1, RG_W)), _full((1, RG_W)), _full((1, RG_W)),
                  _full(bm.shape), _full(lam_re.shape), _full(lam_im.shape), _full(cm.shape),
                  _full((1, S5_W)), _full((S5_W, S5_W)), _full((1, S5_W))],
        out_specs=_rows(RG_W + S5_W),
        out_shape=jax.ShapeDtypeStruct((n_rows, RG_W + S5_W), BF16),
        scratch_shapes=[pltpu.VMEM((HALO + ROWS, RG_W), F32), pltpu.VMEM((ROWS, RG_W), F32),
                        pltpu.VMEM((ROWS, RG_W), F32), pltpu.VMEM((SUBLANES, RG_W), F32),
                        pltpu.VMEM((ROWS, n_state), F32), pltpu.VMEM((SUBLANES, n_state), F32)],
        compiler_params=_seq_params(),
        name="rg_s5",
    )(h, row1(norm_g), w_a, rg_conv_w.astype(F32), row1(rg_conv_b), wg, row1(rg_b_a), row1(rg_b_x), c_rg,
      bm, lam_re, lam_im, cm, row1(s5_d), s5_w_glu.astype(BF16), row1(s5_b_glu))

    wqkv = w_in[:, o_q:o_q + 3 * GDN_W].astype(BF16)
    wz = w_in[:, o_z:o_z + GDN_W].astype(BF16)
    zpad = jnp.zeros((d_model, LANES - GDN_H), w_in.dtype)
    wab = jnp.concatenate([w_in[:, o_a:o_a + GDN_H], zpad, w_in[:, o_b:o_b + GDN_H], zpad], axis=1).astype(BF16)
    lane_pad = lambda v: jnp.pad(v.reshape(1, -1).astype(F32), ((0, 0), (0, LANES - GDN_H)))
    big = lambda: pltpu.VMEM((GDN_H, ROWS, GDN_D), F32)
    y_g = pl.pallas_call(
        _gdn_kernel,
        grid=grid,
        in_specs=[_rows(d_model), _full((1, d_model)), _full(wqkv.shape), _full(wz.shape), _full(wab.shape),
                  _full((CONV_K, 3 * GDN_W)), _full((1, LANES)), _full((1, LANES)), _full((1, GDN_D))],
        out_specs=_rows(GDN_W),
        out_shape=jax.ShapeDtypeStruct((n_rows, GDN_W), BF16),
        scratch_shapes=[pltpu.VMEM((HALO + ROWS, 3 * GDN_W), F32), pltpu.VMEM((ROWS, d_model), BF16),
                        big(), big(), big(), big(), big(), big(), big(),
                        pltpu.VMEM((ROWS, LANES), F32), big(),
                        pltpu.VMEM((SUBLANES * GDN_H, GDN_D, GDN_D), F32)],
        compiler_params=_seq_params(),
        name="gdn",
    )(h, row1(norm_g), wqkv, wz, wab, gdn_conv_w.astype(F32), lane_pad(-jnp.exp(gdn_a_log)),
      lane_pad(gdn_dt_bias), row1(gdn_norm_g))

    wo_a = jnp.concatenate([w_out[0:RG_W], w_out[RG_W + GDN_W:]], axis=0).astype(BF16)
    wo_g = w_out[RG_W:RG_W + GDN_W].astype(BF16)
    d_ple = p_i.shape[1]
    return pl.pallas_call(
        functools.partial(_out_kernel, final=final),
        grid=grid,
        in_specs=[_rows(d_model), _rows(RG_W + S5_W), _rows(GDN_W), _rows(d_ple), _full(wo_a.shape),
                  _full(wo_g.shape), _full((1, d_model)), _full((d_model, d_model)), _full((d_ple, d_model)),
                  _full((1, d_model))],
        out_specs=_rows(d_model),
        out_shape=jax.ShapeDtypeStruct((n_rows, d_model), F32),
        compiler_params=pltpu.CompilerParams(dimension_semantics=("parallel",), vmem_limit_bytes=VMEM_LIMIT),
        name="out_ple",
    )(h, y_a, y_g, p_i, wo_a, wo_g, row1(ple_norm_g), ple_w_gate.astype(BF16), ple_w_proj.astype(BF16),
      row1(final_g))


def kernel(x, p, norm_g, w_in, rg_conv_w, rg_conv_b, rg_w_a, rg_b_a, rg_w_x, rg_b_x, rg_lambda, gdn_conv_w, gdn_a_log, gdn_dt_bias, gdn_norm_g, s5_a_re, s5_a_im, s5_b_re, s5_b_im, s5_c_re, s5_c_im, s5_d, s5_log_dt, s5_w_glu, s5_b_glu, w_out, ple_norm_g, ple_w_gate, ple_w_proj, final_norm_g):
    batch, seq, d_model = x.shape
    depth = p.shape[0]
    assert batch == SUBLANES and seq % TILE_T == 0
    assert x.dtype == F32
    layer_params = (norm_g, w_in, rg_conv_w, rg_conv_b, rg_w_a, rg_b_a, rg_w_x, rg_b_x, rg_lambda,
                    gdn_conv_w, gdn_a_log, gdn_dt_bias, gdn_norm_g, s5_a_re, s5_a_im, s5_b_re, s5_b_im,
                    s5_c_re, s5_c_im, s5_d, s5_log_dt, s5_w_glu, s5_b_glu, w_out, ple_norm_g, ple_w_gate,
                    ple_w_proj)
    h = jnp.swapaxes(x, 0, 1).reshape(seq * batch, d_model)
    p_tm = jnp.swapaxes(p, 1, 2).reshape(depth, seq * batch, p.shape[-1])
    for i in range(depth):
        h = _layer(h, p_tm[i], tuple(a[i] for a in layer_params), i == depth - 1, final_norm_g)
    return jnp.swapaxes(h.reshape(seq, batch, d_model), 0, 1)
```

```python
import functools

import jax
import jax.numpy as jnp
from jax import lax
from jax.experimental import pallas as pl
from jax.experimental.pallas import tpu as pltpu

F32 = jnp.float32
BF16 = jnp.bfloat16

NORM_EPS = 1e-6
RG_C = 8.0
CONV_K = 4

SUBLANES = 8
LANES = 128
TILE_T = 64
ROWS = TILE_T * SUBLANES
HALO = (CONV_K - 1) * SUBLANES

RG_W = 512
S5_W = 512
S5_GROUP = 16
S5_STATE = 64
S5_LANE_GROUPS = LANES // S5_GROUP
S5_BLK_STATE = S5_LANE_GROUPS * S5_STATE
GDN_H = 8
GDN_D = 128
GDN_W = GDN_H * GDN_D

VMEM_LIMIT = 56 * 1024 * 1024


def _rms(x, g):
    ms = jnp.mean(x * x, axis=-1, keepdims=True)
    return x * lax.rsqrt(ms + NORM_EPS) * g


def _sigmoid(x):
    return 1.0 / (1.0 + jnp.exp(-x))


def _silu(x):
    return x * _sigmoid(x)


def _softplus(x):
    return jnp.maximum(x, 0.0) + jnp.log1p(jnp.exp(-jnp.abs(x)))


def _expm1(x):
    u = jnp.exp(x)
    um1 = u - 1.0
    regular = jnp.logical_and(um1 != 0.0, um1 != -1.0)
    v = um1 * x / jnp.log(jnp.where(regular, u, 2.0))
    return jnp.where(um1 == 0.0, x, jnp.where(um1 == -1.0, -1.0, v))


def _dot(a, b):
    return jnp.dot(a.astype(BF16), b.astype(BF16), preferred_element_type=F32)


def _dot_nt(a, b):
    return lax.dot_general(a.astype(BF16), b.astype(BF16), (((1,), (1,)), ((), ())),
                           preferred_element_type=F32)


def _dot_tn(a, b):
    return lax.dot_general(a.astype(BF16), b.astype(BF16), (((0,), (0,)), ((), ())),
                           preferred_element_type=F32)


def _causal_conv(x, buf, cw):
    buf[HALO:HALO + ROWS, :] = x
    y = cw[CONV_K - 1:CONV_K, :] * x
    for k in range(CONV_K - 1):
        off = HALO - (CONV_K - 1 - k) * SUBLANES
        y = y + cw[k:k + 1, :] * buf[off:off + ROWS, :]
    buf[0:HALO, :] = buf[ROWS:ROWS + HALO, :]
    return y


def _rg_s5_kernel(h_ref, ng_ref, w_ref, cw_ref, cb_ref, wg_ref, ba_ref, bx_ref, c_ref,
                  bm_ref, lr_ref, li_ref, cm_ref, d_ref, wglu_ref, bglu_ref,
                  y_ref,
                  xbuf, a_buf, i_buf, hcar, st_buf, scar):
    @pl.when(pl.program_id(0) == 0)
    def _():
        xbuf[0:HALO, :] = jnp.zeros((HALO, RG_W), F32)
        hcar[...] = jnp.zeros_like(hcar)
        scar[...] = jnp.zeros_like(scar)

    hn = _rms(h_ref[...], ng_ref[...]).astype(BF16)
    proj = jnp.dot(hn, w_ref[...], preferred_element_type=F32)
    rg_x = proj[:, 0:RG_W]
    rg_gate = proj[:, RG_W:2 * RG_W]
    s5_u = proj[:, 2 * RG_W:2 * RG_W + S5_W]
    s5_gate = proj[:, 2 * RG_W + S5_W:]

    xr = _causal_conv(rg_x, xbuf, cw_ref[...]) + cb_ref[...]
    pre_a, pre_x = [], []
    for j in range(RG_W // LANES):
        g = _dot(xr[:, j * LANES:(j + 1) * LANES], wg_ref[j])
        pre_a.append(g[:, :LANES])
        pre_x.append(g[:, LANES:])
    r = _sigmoid(jnp.concatenate(pre_a, axis=1) + ba_ref[...])
    gi = _sigmoid(jnp.concatenate(pre_x, axis=1) + bx_ref[...])
    log_a = c_ref[...] * r
    a_buf[...] = jnp.exp(log_a)
    i_buf[...] = jnp.sqrt(-_expm1(2.0 * log_a)) * (gi * xr)

    def rg_step(t, hc):
        r0 = pl.multiple_of(t * SUBLANES, SUBLANES)
        hc = a_buf[pl.ds(r0, SUBLANES), :] * hc + i_buf[pl.ds(r0, SUBLANES), :]
        i_buf[pl.ds(r0, SUBLANES), :] = hc
        return hc

    hcar[...] = lax.fori_loop(0, TILE_T, rg_step, hcar[...], unroll=8)
    y_ref[:, 0:RG_W] = (i_buf[...] * _silu(rg_gate)).astype(y_ref.dtype)

    nblk = S5_W // LANES
    sw = 2 * S5_BLK_STATE
    for j in range(nblk):
        st_buf[:, j * sw:(j + 1) * sw] = _dot(s5_u[:, j * LANES:(j + 1) * LANES], bm_ref[j])
    for j in range(nblk):
        c_re = j * sw
        c_im = j * sw + S5_BLK_STATE
        lr = jnp.broadcast_to(lr_ref[:, j * S5_BLK_STATE:(j + 1) * S5_BLK_STATE], (SUBLANES, S5_BLK_STATE))
        li = jnp.broadcast_to(li_ref[:, j * S5_BLK_STATE:(j + 1) * S5_BLK_STATE], (SUBLANES, S5_BLK_STATE))

        def s5_step(t, carry, c_re=c_re, c_im=c_im, lr=lr, li=li):
            sr, si = carry
            r0 = pl.multiple_of(t * SUBLANES, SUBLANES)
            nr = lr * sr - li * si + st_buf[pl.ds(r0, SUBLANES), c_re:c_re + S5_BLK_STATE]
            ni = lr * si + li * sr + st_buf[pl.ds(r0, SUBLANES), c_im:c_im + S5_BLK_STATE]
            st_buf[pl.ds(r0, SUBLANES), c_re:c_re + S5_BLK_STATE] = nr
            st_buf[pl.ds(r0, SUBLANES), c_im:c_im + S5_BLK_STATE] = ni
            return nr, ni

        sr, si = lax.fori_loop(
            0, TILE_T, s5_step,
            (scar[:, c_re:c_re + S5_BLK_STATE], scar[:, c_im:c_im + S5_BLK_STATE]), unroll=4)
        scar[:, c_re:c_re + S5_BLK_STATE] = sr
        scar[:, c_im:c_im + S5_BLK_STATE] = si
    ys = [_dot(st_buf[:, j * sw:(j + 1) * sw], cm_ref[j]) for j in range(nblk)]
    y5 = jnp.concatenate(ys, axis=1) + d_ref[...] * s5_u
    z5 = jax.nn.gelu(y5)
    y5 = z5 * _sigmoid(_dot(z5, wglu_ref[...]) + bglu_ref[...])
    y_ref[:, RG_W:RG_W + S5_W] = (y5 * _silu(s5_gate)).astype(y_ref.dtype)


N_SC = 8
SC_GC, SC_BRK, SC_RQ, SC_QEG, SC_KDL, SC_RK, SC_BETA, SC_TWC = range(N_SC)
SC_COLS = (SC_GC, SC_BRK, SC_RQ, SC_QEG, SC_KDL)
SC_ROWS = (SC_GC, SC_RK, SC_BETA, SC_TWC)
CONV_RB = 32
GDN_PAR = 2


def _gdn_kernel(h_ref, ng_ref, wqkv_ref, wz_ref, wab_ref, cw_ref, nea_ref, dtb_ref, gn_ref, sel_ref,
                y_ref,
                cbuf, hn_buf, qkv_s, sq_s, sc_s, o_s, st_ref):
    @pl.when(pl.program_id(0) == 0)
    def _():
        cbuf[0:HALO, :] = jnp.zeros((HALO, 3 * GDN_W), F32)
        st_ref[...] = jnp.zeros_like(st_ref)

    hn_buf[...] = _rms(h_ref[...], ng_ref[...]).astype(BF16)

    ab = jnp.dot(hn_buf[...], wab_ref[...], preferred_element_type=F32)
    g = nea_ref[...] * _softplus(ab[:, :LANES] + dtb_ref[...])
    beta = _sigmoid(ab[:, LANES:])
    gc = g
    d = 1
    while d < TILE_T:
        sh = d * SUBLANES
        gc = gc + jnp.concatenate([jnp.zeros((sh, LANES), F32), gc[:ROWS - sh, :]], axis=0)
        d *= 2
    g_last = jnp.broadcast_to(gc[ROWS - SUBLANES:, :][None], (TILE_T, SUBLANES, LANES)).reshape(ROWS, LANES)

    for c in range(3):
        cbuf[HALO:HALO + ROWS, c * GDN_W:(c + 1) * GDN_W] = jnp.dot(
            hn_buf[...], wqkv_ref[:, c * GDN_W:(c + 1) * GDN_W], preferred_element_type=F32)

    def conv_rows(i, carry):
        r0 = pl.multiple_of(i * CONV_RB, CONV_RB)
        for c in range(3 * GDN_H):
            cols = slice(c * GDN_D, (c + 1) * GDN_D)
            y = cw_ref[CONV_K - 1:CONV_K, cols] * cbuf[pl.ds(r0 + HALO, CONV_RB), cols]
            for k in range(CONV_K - 1):
                off = HALO - (CONV_K - 1 - k) * SUBLANES
                y = y + cw_ref[k:k + 1, cols] * cbuf[pl.ds(r0 + off, CONV_RB), cols]
            act = _silu(y)
            qkv_s[c, pl.ds(r0, CONV_RB), :] = act
            if c < 2 * GDN_H:
                sq_s[pl.ds(r0, CONV_RB), cols] = (act * act).astype(BF16)
        return carry

    lax.fori_loop(0, ROWS // CONV_RB, conv_rows, 0)
    cbuf[0:HALO, :] = cbuf[ROWS:ROWS + HALO, :]

    ss_q = jnp.dot(sq_s[:, 0:GDN_W], sel_ref[...], preferred_element_type=F32)
    ss_k = jnp.dot(sq_s[:, GDN_W:2 * GDN_W], sel_ref[...], preferred_element_type=F32)
    rq = lax.rsqrt(ss_q + NORM_EPS) * (GDN_D ** -0.5)
    rk = lax.rsqrt(ss_k + NORM_EPS)
    egc = jnp.exp(gc)
    sc_s[SC_GC] = gc
    sc_s[SC_BRK] = beta * rk
    sc_s[SC_RQ] = rq
    sc_s[SC_QEG] = rq * egc
    sc_s[SC_KDL] = rk * jnp.exp(g_last - gc)
    sc_s[SC_RK] = rk
    sc_s[SC_BETA] = beta
    sc_s[SC_TWC] = beta * rk * egc

    row = lax.broadcasted_iota(jnp.int32, (TILE_T, TILE_T), 0)
    col = lax.broadcasted_iota(jnp.int32, (TILE_T, TILE_T), 1)
    causal = row >= col
    strict = row > col
    eye = (row == col).astype(F32)

    def per_group(i, carry):
        chains = [(i * GDN_PAR + s, h) for s in range(GDN_PAR) for h in range(GDN_H)]
        colv, rowv = [], []
        for s in range(GDN_PAR):
            rows = pl.ds(i * GDN_PAR + s, TILE_T, stride=SUBLANES)
            colv.append({n: sc_s[n, rows, :] for n in SC_COLS})
            pad = jnp.zeros((TILE_T, LANES), F32)
            rowv.append({n: jnp.concatenate([sc_s[n, rows, :], pad], axis=0).T for n in SC_ROWS})
        ck = lambda s, n, h: colv[s][n][:, h:h + 1]
        rw = lambda s, n, h: rowv[s][n][h:h + 1, 0:TILE_T]

        q_b, k_b, v_b = [], [], []
        for b, h in chains:
            rows = pl.ds(b, TILE_T, stride=SUBLANES)
            q_b.append(qkv_s[h, rows, :].astype(BF16))
            k_b.append(qkv_s[GDN_H + h, rows, :].astype(BF16))
            v_b.append(qkv_s[2 * GDN_H + h, rows, :].astype(BF16))
        n_ch = len(chains)
        kk = [_dot_nt(jnp.concatenate([k_b[c], q_b[c]], axis=0), k_b[c]) for c in range(n_ch)]
        a_qk, neg = [], []
        for c, (b, h) in enumerate(chains):
            s = c // GDN_H
            diff = ck(s, SC_GC, h) - rw(s, SC_GC, h)
            dk = jnp.where(causal, jnp.exp(jnp.where(causal, diff, 0.0)), 0.0) * rw(s, SC_RK, h)
            neg.append(jnp.where(strict, -(kk[c][:TILE_T] * ck(s, SC_BRK, h)) * dk, 0.0))
            a_qk.append(kk[c][TILE_T:] * ck(s, SC_RQ, h) * dk)
        tinv = [eye + n for n in neg]
        pw = [_dot(n, n) for n in neg]
        span = 2
        while span < TILE_T:
            tinv = [t + _dot(t, p) for t, p in zip(tinv, pw)]
            span *= 2
            if span < TILE_T:
                pw = [_dot(p, p) for p in pw]
        u, w = [], []
        for c, (b, h) in enumerate(chains):
            s = c // GDN_H
            u.append(_dot(tinv[c] * rw(s, SC_BETA, h), v_b[c]))
            w.append(_dot(tinv[c] * rw(s, SC_TWC, h), k_b[c]))
        state = [st_ref[b * GDN_H + h] for b, h in chains]
        ws_qs = [_dot(jnp.concatenate([w[c].astype(BF16), q_b[c]], axis=0), state[c]) for c in range(n_ch)]
        v_new = [u[c] - ws_qs[c][:TILE_T] for c in range(n_ch)]
        for c, (b, h) in enumerate(chains):
            s = c // GDN_H
            o_s[h, pl.ds(b, TILE_T, stride=SUBLANES), :] = (
                ws_qs[c][TILE_T:] * ck(s, SC_QEG, h) + _dot(a_qk[c], v_new[c]))
        for c, (b, h) in enumerate(chains):
            s = c // GDN_H
            e_last = jnp.exp(rowv[s][SC_GC][h:h + 1, TILE_T - 1:TILE_T])
            st_ref[b * GDN_H + h] = state[c] * e_last + _dot_tn(k_b[c], v_new[c] * ck(s, SC_KDL, h))
        return carry

    lax.fori_loop(0, SUBLANES // GDN_PAR, per_group, 0)

    z = jnp.dot(hn_buf[...], wz_ref[...], preferred_element_type=F32)
    for h in range(GDN_H):
        sl = slice(h * GDN_D, (h + 1) * GDN_D)
        y_ref[:, sl] = (_rms(o_s[h], gn_ref[...]) * _silu(z[:, sl])).astype(y_ref.dtype)


def _out_kernel(h_ref, ya_ref, yg_ref, p_ref, woa_ref, wog_ref, png_ref, wgate_ref, wproj_ref, fg_ref,
                o_ref, *, final):
    h = h_ref[...] + jnp.dot(ya_ref[...], woa_ref[...], preferred_element_type=F32) \
        + jnp.dot(yg_ref[...], wog_ref[...], preferred_element_type=F32)
    gate = _sigmoid(_dot(_rms(h, png_ref[...]), wgate_ref[...]))
    h = h + gate * _dot(p_ref[...], wproj_ref[...])
    o_ref[...] = _rms(h, fg_ref[...]) if final else h


def _full(shape):
    nd = len(shape)
    return pl.BlockSpec(shape, lambda i: (0,) * nd)


def _rows(width):
    return pl.BlockSpec((ROWS, width), lambda i: (i, 0))


def _seq_params():
    return pltpu.CompilerParams(dimension_semantics=("arbitrary",), vmem_limit_bytes=VMEM_LIMIT)


def _block_diag(blocks):
    n, r, c = blocks.shape
    eye = jnp.eye(n, dtype=blocks.dtype)
    return jnp.einsum('grc,gh->grhc', blocks, eye).reshape(n * r, n * c)


def _s5_operators(a_re, a_im, b_re, b_im, c_re, c_im, log_dt):
    lam = lax.complex(a_re, a_im)
    dt = jnp.exp(log_dt)[:, None]
    lam_bar = jnp.exp(lam * dt)
    b_bar = ((lam_bar - 1.0) / lam)[..., None] * lax.complex(b_re, b_im)
    nblk = S5_W // LANES
    bt = jnp.swapaxes(b_bar, 1, 2).reshape(nblk, S5_LANE_GROUPS, S5_GROUP, S5_STATE)
    bm = jnp.concatenate([jax.vmap(_block_diag)(jnp.real(bt)), jax.vmap(_block_diag)(jnp.imag(bt))], axis=-1)
    ct = jnp.swapaxes(lax.complex(c_re, c_im), 1, 2).reshape(nblk, S5_LANE_GROUPS, S5_STATE, S5_GROUP)
    cm = jnp.concatenate([jax.vmap(_block_diag)(jnp.real(ct)), -jax.vmap(_block_diag)(jnp.imag(ct))], axis=1)
    return (bm.astype(BF16), jnp.real(lam_bar).reshape(1, -1), jnp.imag(lam_bar).reshape(1, -1),
            cm.astype(BF16))


def _layer(h, p_i, prm, final, final_g):
    n_rows = h.shape[0]
    grid = (n_rows // ROWS,)
    d_model = h.shape[1]
    (norm_g, w_in, rg_conv_w, rg_conv_b, rg_w_a, rg_b_a, rg_w_x, rg_b_x, rg_lambda,
     gdn_conv_w, gdn_a_log, gdn_dt_bias, gdn_norm_g, s5_a_re, s5_a_im, s5_b_re, s5_b_im,
     s5_c_re, s5_c_im, s5_d, s5_log_dt, s5_w_glu, s5_b_glu, w_out, ple_norm_g, ple_w_gate,
     ple_w_proj) = prm

    o_rgx, o_rgg = 0, RG_W
    o_q = 2 * RG_W
    o_z = o_q + 3 * GDN_W
    o_b = o_z + GDN_W
    o_a = o_b + GDN_H
    o_u = o_a + GDN_H
    o_sg = o_u + S5_W
    row1 = lambda v: v.reshape(1, -1).astype(F32)

    w_a = jnp.concatenate([w_in[:, o_rgx:o_rgx + 2 * RG_W], w_in[:, o_u:o_u + 2 * S5_W]], axis=1).astype(BF16)
    nb = RG_W // LANES
    per = rg_w_a.shape[0] // nb
    wg = jnp.stack([jnp.concatenate([_block_diag(rg_w_a[j * per:(j + 1) * per]),
                                     _block_diag(rg_w_x[j * per:(j + 1) * per])], axis=1)
                    for j in range(nb)]).astype(BF16)
    c_rg = row1(-RG_C * jax.nn.softplus(-rg_lambda))
    bm, lam_re, lam_im, cm = _s5_operators(s5_a_re, s5_a_im, s5_b_re, s5_b_im, s5_c_re, s5_c_im, s5_log_dt)
    n_state = 2 * S5_BLK_STATE * (S5_W // LANES)
    y_a = pl.pallas_call(
        _rg_s5_kernel,
        grid=grid,
        in_specs=[_rows(d_model), _full((1, d_model)), _full(w_a.shape), _full((CONV_K, RG_W)),
                  _full((1, RG_W)), _full(wg.shape), _full((1, RG_W)), _full((1, RG_W)), _full((1, RG_W)),
                  _full(bm.shape), _full(lam_re.shape), _full(lam_im.shape), _full(cm.shape),
                  _full((1, S5_W)), _full((S5_W, S5_W)), _full((1, S5_W))],
        out_specs=_rows(RG_W + S5_W),
        out_shape=jax.ShapeDtypeStruct((n_rows, RG_W + S5_W), BF16),
        scratch_shapes=[pltpu.VMEM((HALO + ROWS, RG_W), F32), pltpu.VMEM((ROWS, RG_W), F32),
                        pltpu.VMEM((ROWS, RG_W), F32), pltpu.VMEM((SUBLANES, RG_W), F32),
                        pltpu.VMEM((ROWS, n_state), F32), pltpu.VMEM((SUBLANES, n_state), F32)],
        compiler_params=_seq_params(),
        name="rg_s5",
    )(h, row1(norm_g), w_a, rg_conv_w.astype(F32), row1(rg_conv_b), wg, row1(rg_b_a), row1(rg_b_x), c_rg,
      bm, lam_re, lam_im, cm, row1(s5_d), s5_w_glu.astype(BF16), row1(s5_b_glu))

    wqkv = w_in[:, o_q:o_q + 3 * GDN_W].astype(BF16)
    wz = w_in[:, o_z:o_z + GDN_W].astype(BF16)
    zpad = jnp.zeros((d_model, LANES - GDN_H), w_in.dtype)
    wab = jnp.concatenate([w_in[:, o_a:o_a + GDN_H], zpad, w_in[:, o_b:o_b + GDN_H], zpad], axis=1).astype(BF16)
    lane_pad = lambda v: jnp.pad(v.reshape(1, -1).astype(F32), ((0, 0), (0, LANES - GDN_H)))
    sel = (jnp.arange(GDN_W)[:, None] // GDN_D == jnp.arange(LANES)[None, :]).astype(BF16)
    y_g = pl.pallas_call(
        _gdn_kernel,
        grid=grid,
        in_specs=[_rows(d_model), _full((1, d_model)), _full(wqkv.shape), _full(wz.shape), _full(wab.shape),
                  _full((CONV_K, 3 * GDN_W)), _full((1, LANES)), _full((1, LANES)), _full((1, GDN_D)),
                  _full(sel.shape)],
        out_specs=_rows(GDN_W),
        out_shape=jax.ShapeDtypeStruct((n_rows, GDN_W), BF16),
        scratch_shapes=[pltpu.VMEM((HALO + ROWS, 3 * GDN_W), F32), pltpu.VMEM((ROWS, d_model), BF16),
                        pltpu.VMEM((3 * GDN_H, ROWS, GDN_D), F32),
                        pltpu.VMEM((ROWS, 2 * GDN_W), BF16),
                        pltpu.VMEM((N_SC, ROWS, LANES), F32),
                        pltpu.VMEM((GDN_H, ROWS, GDN_D), F32),
                        pltpu.VMEM((SUBLANES * GDN_H, GDN_D, GDN_D), F32)],
        compiler_params=_seq_params(),
        name="gdn",
    )(h, row1(norm_g), wqkv, wz, wab, gdn_conv_w.astype(F32), lane_pad(-jnp.exp(gdn_a_log)),
      lane_pad(gdn_dt_bias), row1(gdn_norm_g), sel)

    wo_a = jnp.concatenate([w_out[0:RG_W], w_out[RG_W + GDN_W:]], axis=0).astype(BF16)
    wo_g = w_out[RG_W:RG_W + GDN_W].astype(BF16)
    d_ple = p_i.shape[1]
    return pl.pallas_call(
        functools.partial(_out_kernel, final=final),
        grid=grid,
        in_specs=[_rows(d_model), _rows(RG_W + S5_W), _rows(GDN_W), _rows(d_ple), _full(wo_a.shape),
                  _full(wo_g.shape), _full((1, d_model)), _full((d_model, d_model)), _full((d_ple, d_model)),
                  _full((1, d_model))],
        out_specs=_rows(d_model),
        out_shape=jax.ShapeDtypeStruct((n_rows, d_model), F32),
        compiler_params=pltpu.CompilerParams(dimension_semantics=("parallel",), vmem_limit_bytes=VMEM_LIMIT),
        name="out_ple",
    )(h, y_a, y_g, p_i, wo_a, wo_g, row1(ple_norm_g), ple_w_gate.astype(BF16), ple_w_proj.astype(BF16),
      row1(final_g))


def kernel(x, p, norm_g, w_in, rg_conv_w, rg_conv_b, rg_w_a, rg_b_a, rg_w_x, rg_b_x, rg_lambda, gdn_conv_w, gdn_a_log, gdn_dt_bias, gdn_norm_g, s5_a_re, s5_a_im, s5_b_re, s5_b_im, s5_c_re, s5_c_im, s5_d, s5_log_dt, s5_w_glu, s5_b_glu, w_out, ple_norm_g, ple_w_gate, ple_w_proj, final_norm_g):
    batch, seq, d_model = x.shape
    depth = p.shape[0]
    assert batch == SUBLANES and seq % TILE_T == 0
    assert x.dtype == F32
    layer_params = (norm_g, w_in, rg_conv_w, rg_conv_b, rg_w_a, rg_b_a, rg_w_x, rg_b_x, rg_lambda,
                    gdn_conv_w, gdn_a_log, gdn_dt_bias, gdn_norm_g, s5_a_re, s5_a_im, s5_b_re, s5_b_im,
                    s5_c_re, s5_c_im, s5_d, s5_log_dt, s5_w_glu, s5_b_glu, w_out, ple_norm_g, ple_w_gate,
                    ple_w_proj)
    h = jnp.swapaxes(x, 0, 1).reshape(seq * batch, d_model)
    p_tm = jnp.swapaxes(p, 1, 2).reshape(depth, seq * batch, p.shape[-1])
    for i in range(depth):
        h = _layer(h, p_tm[i], tuple(a[i] for a in layer_params), i == depth - 1, final_norm_g)
    return jnp.swapaxes(h.reshape(seq, batch, d_model), 0, 1)
```

```python
import functools

import jax
import jax.numpy as jnp
from jax import lax
from jax.experimental import pallas as pl
from jax.experimental.pallas import tpu as pltpu

F32 = jnp.float32
BF16 = jnp.bfloat16

NORM_EPS = 1e-6
RG_C = 8.0
CONV_K = 4

SUBLANES = 8
LANES = 128
TILE_T = 64
ROWS = TILE_T * SUBLANES
HALO = (CONV_K - 1) * SUBLANES

RG_W = 512
S5_W = 512
S5_GROUP = 16
S5_STATE = 64
S5_LANE_GROUPS = LANES // S5_GROUP
S5_BLK_STATE = S5_LANE_GROUPS * S5_STATE
GDN_H = 8
GDN_D = 128
GDN_W = GDN_H * GDN_D

VMEM_LIMIT = 56 * 1024 * 1024


def _rms(x, g):
    ms = jnp.mean(x * x, axis=-1, keepdims=True)
    return x * lax.rsqrt(ms + NORM_EPS) * g


def _sigmoid(x):
    return 1.0 / (1.0 + jnp.exp(-x))


def _silu(x):
    return x * _sigmoid(x)


def _silu_half(hx):
    return hx + hx * jnp.tanh(hx)


def _softplus(x):
    return jnp.maximum(x, 0.0) + jnp.log1p(jnp.exp(-jnp.abs(x)))


def _expm1(x):
    u = jnp.exp(x)
    um1 = u - 1.0
    regular = jnp.logical_and(um1 != 0.0, um1 != -1.0)
    v = um1 * x / jnp.log(jnp.where(regular, u, 2.0))
    return jnp.where(um1 == 0.0, x, jnp.where(um1 == -1.0, -1.0, v))


def _dot(a, b):
    return jnp.dot(a.astype(BF16), b.astype(BF16), preferred_element_type=F32)


def _dot_nt(a, b):
    return lax.dot_general(a.astype(BF16), b.astype(BF16), (((1,), (1,)), ((), ())),
                           preferred_element_type=F32)


def _dot_tn(a, b):
    return lax.dot_general(a.astype(BF16), b.astype(BF16), (((0,), (0,)), ((), ())),
                           preferred_element_type=F32)


def _causal_conv(x, buf, cw):
    buf[HALO:HALO + ROWS, :] = x
    y = cw[CONV_K - 1:CONV_K, :] * x
    for k in range(CONV_K - 1):
        off = HALO - (CONV_K - 1 - k) * SUBLANES
        y = y + cw[k:k + 1, :] * buf[off:off + ROWS, :]
    buf[0:HALO, :] = buf[ROWS:ROWS + HALO, :]
    return y


def _to_time_major(src_ref, chunk_buf):
    n_chunks = src_ref.shape[-1] // LANES
    for b in range(SUBLANES):
        for c in range(n_chunks):
            chunk_buf[c, pl.ds(b, TILE_T, stride=SUBLANES), :] = src_ref[b, :, c * LANES:(c + 1) * LANES]
    return jnp.concatenate([chunk_buf[c] for c in range(n_chunks)], axis=1)


def _from_time_major(val, chunk_buf, dst_ref):
    n_chunks = val.shape[-1] // LANES
    for c in range(n_chunks):
        chunk_buf[c] = val[:, c * LANES:(c + 1) * LANES]
    for b in range(SUBLANES):
        for c in range(n_chunks):
            dst_ref[b, :, c * LANES:(c + 1) * LANES] = chunk_buf[c, pl.ds(b, TILE_T, stride=SUBLANES), :]


def _rg_s5_kernel(h_ref, ng_ref, w_ref, cw_ref, cb_ref, wg_ref, ba_ref, bx_ref, c_ref,
                  bm_ref, lr_ref, li_ref, cm_ref, d_ref, wglu_ref, bglu_ref,
                  y_ref, *rest, first):
    if first:
        htm_ref, xbuf, a_buf, i_buf, hcar, st_buf, scar, chunk_buf = rest
    else:
        xbuf, a_buf, i_buf, hcar, st_buf, scar = rest

    @pl.when(pl.program_id(0) == 0)
    def _():
        xbuf[0:HALO, :] = jnp.zeros((HALO, RG_W), F32)
        hcar[...] = jnp.zeros_like(hcar)
        scar[...] = jnp.zeros_like(scar)

    if first:
        h = _to_time_major(h_ref, chunk_buf)
        htm_ref[...] = h
    else:
        h = h_ref[...]
    hn = _rms(h, ng_ref[...]).astype(BF16)
    proj = jnp.dot(hn, w_ref[...], preferred_element_type=F32)
    rg_x = proj[:, 0:RG_W]
    rg_gate = proj[:, RG_W:2 * RG_W]
    s5_u = proj[:, 2 * RG_W:2 * RG_W + S5_W]
    s5_gate = proj[:, 2 * RG_W + S5_W:]

    xr = _causal_conv(rg_x, xbuf, cw_ref[...]) + cb_ref[...]
    pre_a, pre_x = [], []
    for j in range(RG_W // LANES):
        g = _dot(xr[:, j * LANES:(j + 1) * LANES], wg_ref[j])
        pre_a.append(g[:, :LANES])
        pre_x.append(g[:, LANES:])
    r = _sigmoid(jnp.concatenate(pre_a, axis=1) + ba_ref[...])
    gi = _sigmoid(jnp.concatenate(pre_x, axis=1) + bx_ref[...])
    log_a = c_ref[...] * r
    a_buf[...] = jnp.exp(log_a)
    i_buf[...] = jnp.sqrt(-_expm1(2.0 * log_a)) * (gi * xr)

    def rg_step(t, hc):
        r0 = pl.multiple_of(t * SUBLANES, SUBLANES)
        hc = a_buf[pl.ds(r0, SUBLANES), :] * hc + i_buf[pl.ds(r0, SUBLANES), :]
        i_buf[pl.ds(r0, SUBLANES), :] = hc
        return hc

    hcar[...] = lax.fori_loop(0, TILE_T, rg_step, hcar[...], unroll=8)
    y_ref[:, 0:RG_W] = (i_buf[...] * _silu_half(rg_gate)).astype(y_ref.dtype)

    nblk = S5_W // LANES
    sw = 2 * S5_BLK_STATE
    for j in range(nblk):
        st_buf[:, j * sw:(j + 1) * sw] = _dot(s5_u[:, j * LANES:(j + 1) * LANES], bm_ref[j])
    for j in range(nblk):
        c_re = j * sw
        c_im = j * sw + S5_BLK_STATE
        lr = jnp.broadcast_to(lr_ref[:, j * S5_BLK_STATE:(j + 1) * S5_BLK_STATE], (SUBLANES, S5_BLK_STATE))
        li = jnp.broadcast_to(li_ref[:, j * S5_BLK_STATE:(j + 1) * S5_BLK_STATE], (SUBLANES, S5_BLK_STATE))

        def s5_step(t, carry, c_re=c_re, c_im=c_im, lr=lr, li=li):
            sr, si = carry
            r0 = pl.multiple_of(t * SUBLANES, SUBLANES)
            nr = lr * sr - li * si + st_buf[pl.ds(r0, SUBLANES), c_re:c_re + S5_BLK_STATE]
            ni = lr * si + li * sr + st_buf[pl.ds(r0, SUBLANES), c_im:c_im + S5_BLK_STATE]
            st_buf[pl.ds(r0, SUBLANES), c_re:c_re + S5_BLK_STATE] = nr
            st_buf[pl.ds(r0, SUBLANES), c_im:c_im + S5_BLK_STATE] = ni
            return nr, ni

        sr, si = lax.fori_loop(
            0, TILE_T, s5_step,
            (scar[:, c_re:c_re + S5_BLK_STATE], scar[:, c_im:c_im + S5_BLK_STATE]), unroll=True)
        scar[:, c_re:c_re + S5_BLK_STATE] = sr
        scar[:, c_im:c_im + S5_BLK_STATE] = si
    ys = [_dot(st_buf[:, j * sw:(j + 1) * sw], cm_ref[j]) for j in range(nblk)]
    y5 = jnp.concatenate(ys, axis=1) + d_ref[...] * s5_u
    z5 = jax.nn.gelu(y5)
    y5 = z5 * _sigmoid(_dot(z5, wglu_ref[...]) + bglu_ref[...])
    y_ref[:, RG_W:RG_W + S5_W] = (y5 * _silu_half(s5_gate)).astype(y_ref.dtype)


N_SC = 8
SC_GC, SC_BRK, SC_RQ, SC_QEG, SC_KDL, SC_RK, SC_BETA, SC_TWC = range(N_SC)
SC_COLS = (SC_GC, SC_BRK, SC_RQ, SC_QEG, SC_KDL)
SC_ROWS = (SC_GC, SC_RK, SC_BETA, SC_TWC)
CONV_RB = 32
GDN_PAR = 4


def _gdn_kernel(h_ref, ng_ref, wqkv_ref, wz_ref, wab_ref, cw_ref, nea_ref, dtb_ref, gn_ref, sel_ref,
                y_ref,
                cbuf, hn_buf, qkv_s, sq_s, sc_s, o_s, st_ref):
    @pl.when(pl.program_id(0) == 0)
    def _():
        cbuf[0:HALO, :] = jnp.zeros((HALO, 3 * GDN_W), F32)
        st_ref[...] = jnp.zeros_like(st_ref)

    hn_buf[...] = _rms(h_ref[...], ng_ref[...]).astype(BF16)

    ab = jnp.dot(hn_buf[...], wab_ref[...], preferred_element_type=F32)
    g = nea_ref[...] * _softplus(ab[:, :LANES] + dtb_ref[...])
    beta = _sigmoid(ab[:, LANES:])
    gc = g
    d = 1
    while d < TILE_T:
        sh = d * SUBLANES
        gc = gc + jnp.concatenate([jnp.zeros((sh, LANES), F32), gc[:ROWS - sh, :]], axis=0)
        d *= 2
    g_last = jnp.broadcast_to(gc[ROWS - SUBLANES:, :][None], (TILE_T, SUBLANES, LANES)).reshape(ROWS, LANES)

    for c in range(3):
        cbuf[HALO:HALO + ROWS, c * GDN_W:(c + 1) * GDN_W] = jnp.dot(
            hn_buf[...], wqkv_ref[:, c * GDN_W:(c + 1) * GDN_W], preferred_element_type=F32)

    def conv_rows(i, carry):
        r0 = pl.multiple_of(i * CONV_RB, CONV_RB)
        for c in range(3 * GDN_H):
            cols = slice(c * GDN_D, (c + 1) * GDN_D)
            y = cw_ref[CONV_K - 1:CONV_K, cols] * cbuf[pl.ds(r0 + HALO, CONV_RB), cols]
            for k in range(CONV_K - 1):
                off = HALO - (CONV_K - 1 - k) * SUBLANES
                y = y + cw_ref[k:k + 1, cols] * cbuf[pl.ds(r0 + off, CONV_RB), cols]
            act = _silu_half(y)
            qkv_s[c, pl.ds(r0, CONV_RB), :] = act
            if c < 2 * GDN_H:
                sq_s[pl.ds(r0, CONV_RB), cols] = (act * act).astype(BF16)
        return carry

    lax.fori_loop(0, ROWS // CONV_RB, conv_rows, 0)
    cbuf[0:HALO, :] = cbuf[ROWS:ROWS + HALO, :]

    ss_q = jnp.dot(sq_s[:, 0:GDN_W], sel_ref[...], preferred_element_type=F32)
    ss_k = jnp.dot(sq_s[:, GDN_W:2 * GDN_W], sel_ref[...], preferred_element_type=F32)
    rq = lax.rsqrt(ss_q + NORM_EPS) * (GDN_D ** -0.5)
    rk = lax.rsqrt(ss_k + NORM_EPS)
    egc = jnp.exp(gc)
    sc_s[SC_GC] = gc
    sc_s[SC_BRK] = beta * rk
    sc_s[SC_RQ] = rq
    sc_s[SC_QEG] = rq * egc
    sc_s[SC_KDL] = rk * jnp.exp(g_last - gc)
    sc_s[SC_RK] = rk
    sc_s[SC_BETA] = beta
    sc_s[SC_TWC] = beta * rk * egc

    row = lax.broadcasted_iota(jnp.int32, (TILE_T, TILE_T), 0)
    col = lax.broadcasted_iota(jnp.int32, (TILE_T, TILE_T), 1)
    causal = row >= col
    strict = row > col
    eye = (row == col).astype(F32)

    def per_group(i, carry):
        chains = [(i * GDN_PAR + s, h) for s in range(GDN_PAR) for h in range(GDN_H)]
        colv, rowv = [], []
        for s in range(GDN_PAR):
            rows = pl.ds(i * GDN_PAR + s, TILE_T, stride=SUBLANES)
            colv.append({n: sc_s[n, rows, :] for n in SC_COLS})
            pad = jnp.zeros((TILE_T, LANES), F32)
            rowv.append({n: jnp.concatenate([sc_s[n, rows, :], pad], axis=0).T for n in SC_ROWS})
        ck = lambda s, n, h: colv[s][n][:, h:h + 1]
        rw = lambda s, n, h: rowv[s][n][h:h + 1, 0:TILE_T]

        q_b, k_b, v_b = [], [], []
        for b, h in chains:
            rows = pl.ds(b, TILE_T, stride=SUBLANES)
            q_b.append(qkv_s[h, rows, :].astype(BF16))
            k_b.append(qkv_s[GDN_H + h, rows, :].astype(BF16))
            v_b.append(qkv_s[2 * GDN_H + h, rows, :].astype(BF16))
        n_ch = len(chains)
        kk = [_dot_nt(jnp.concatenate([k_b[c], q_b[c]], axis=0), k_b[c]) for c in range(n_ch)]
        a_qk, neg = [], []
        for c, (b, h) in enumerate(chains):
            s = c // GDN_H
            diff = ck(s, SC_GC, h) - rw(s, SC_GC, h)
            dk = jnp.where(causal, jnp.exp(jnp.where(causal, diff, 0.0)), 0.0) * rw(s, SC_RK, h)
            neg.append(jnp.where(strict, -(kk[c][:TILE_T] * ck(s, SC_BRK, h)) * dk, 0.0))
            a_qk.append(kk[c][TILE_T:] * ck(s, SC_RQ, h) * dk)
        tinv = [eye + n for n in neg]
        pw = [_dot(n, n) for n in neg]
        span = 2
        while span < TILE_T:
            tinv = [t + _dot(t, p) for t, p in zip(tinv, pw)]
            span *= 2
            if span < TILE_T:
                pw = [_dot(p, p) for p in pw]
        u, w = [], []
        for c, (b, h) in enumerate(chains):
            s = c // GDN_H
            u.append(_dot(tinv[c] * rw(s, SC_BETA, h), v_b[c]))
            w.append(_dot(tinv[c] * rw(s, SC_TWC, h), k_b[c]))
        state = [st_ref[b * GDN_H + h] for b, h in chains]
        ws_qs = [_dot(jnp.concatenate([w[c].astype(BF16), q_b[c]], axis=0), state[c]) for c in range(n_ch)]
        v_new = [u[c] - ws_qs[c][:TILE_T] for c in range(n_ch)]
        for c, (b, h) in enumerate(chains):
            s = c // GDN_H
            o_s[h, pl.ds(b, TILE_T, stride=SUBLANES), :] = (
                ws_qs[c][TILE_T:] * ck(s, SC_QEG, h) + _dot(a_qk[c], v_new[c]))
        for c, (b, h) in enumerate(chains):
            s = c // GDN_H
            e_last = jnp.exp(rowv[s][SC_GC][h:h + 1, TILE_T - 1:TILE_T])
            st_ref[b * GDN_H + h] = state[c] * e_last + _dot_tn(k_b[c], v_new[c] * ck(s, SC_KDL, h))
        return carry

    lax.fori_loop(0, SUBLANES // GDN_PAR, per_group, 0)

    z = jnp.dot(hn_buf[...], wz_ref[...], preferred_element_type=F32)
    for h in range(GDN_H):
        sl = slice(h * GDN_D, (h + 1) * GDN_D)
        y_ref[:, sl] = (_rms(o_s[h], gn_ref[...]) * _silu_half(z[:, sl])).astype(y_ref.dtype)


def _out_kernel(h_ref, ya_ref, yg_ref, p_ref, woa_ref, wog_ref, png_ref, wgate_ref, wproj_ref, fg_ref,
                o_ref, p_buf, *rest, final):
    h = h_ref[...] + jnp.dot(ya_ref[...], woa_ref[...], preferred_element_type=F32) \
        + jnp.dot(yg_ref[...], wog_ref[...], preferred_element_type=F32)
    gate = _sigmoid(_dot(_rms(h, png_ref[...]), wgate_ref[...]))
    h = h + gate * _dot(_to_time_major(p_ref, p_buf), wproj_ref[...])
    if final:
        _from_time_major(_rms(h, fg_ref[...]), rest[0], o_ref)
    else:
        o_ref[...] = h


def _full(shape):
    nd = len(shape)
    return pl.BlockSpec(shape, lambda i: (0,) * nd)


def _rows(width):
    return pl.BlockSpec((ROWS, width), lambda i: (i, 0))


def _seq_params():
    return pltpu.CompilerParams(dimension_semantics=("arbitrary",), vmem_limit_bytes=VMEM_LIMIT)


def _block_diag(blocks):
    n, r, c = blocks.shape
    eye = jnp.eye(n, dtype=blocks.dtype)
    return jnp.einsum('grc,gh->grhc', blocks, eye).reshape(n * r, n * c)


def _s5_operators(a_re, a_im, b_re, b_im, c_re, c_im, log_dt):
    lam = lax.complex(a_re, a_im)
    dt = jnp.exp(log_dt)[:, None]
    lam_bar = jnp.exp(lam * dt)
    b_bar = ((lam_bar - 1.0) / lam)[..., None] * lax.complex(b_re, b_im)
    nblk = S5_W // LANES
    bt = jnp.swapaxes(b_bar, 1, 2).reshape(nblk, S5_LANE_GROUPS, S5_GROUP, S5_STATE)
    bm = jnp.concatenate([jax.vmap(_block_diag)(jnp.real(bt)), jax.vmap(_block_diag)(jnp.imag(bt))], axis=-1)
    ct = jnp.swapaxes(lax.complex(c_re, c_im), 1, 2).reshape(nblk, S5_LANE_GROUPS, S5_STATE, S5_GROUP)
    cm = jnp.concatenate([jax.vmap(_block_diag)(jnp.real(ct)), -jax.vmap(_block_diag)(jnp.imag(ct))], axis=1)
    return (bm.astype(BF16), jnp.real(lam_bar).reshape(1, -1), jnp.imag(lam_bar).reshape(1, -1),
            cm.astype(BF16))


def _seq_block(width):
    return pl.BlockSpec((SUBLANES, TILE_T, width), lambda i: (0, i, 0))


def _layer(h, p_i, prm, first, final, final_g):
    d_model = h.shape[-1]
    n_rows = h.shape[0] * h.shape[1] if first else h.shape[0]
    grid = (n_rows // ROWS,)
    (norm_g, w_in, rg_conv_w, rg_conv_b, rg_w_a, rg_b_a, rg_w_x, rg_b_x, rg_lambda,
     gdn_conv_w, gdn_a_log, gdn_dt_bias, gdn_norm_g, s5_a_re, s5_a_im, s5_b_re, s5_b_im,
     s5_c_re, s5_c_im, s5_d, s5_log_dt, s5_w_glu, s5_b_glu, w_out, ple_norm_g, ple_w_gate,
     ple_w_proj) = prm

    o_rgx, o_rgg = 0, RG_W
    o_q = 2 * RG_W
    o_z = o_q + 3 * GDN_W
    o_b = o_z + GDN_W
    o_a = o_b + GDN_H
    o_u = o_a + GDN_H
    o_sg = o_u + S5_W
    row1 = lambda v: v.reshape(1, -1).astype(F32)

    w_a = jnp.concatenate([w_in[:, o_rgx:o_rgx + RG_W], 0.5 * w_in[:, o_rgg:o_rgg + RG_W],
                           w_in[:, o_u:o_u + S5_W], 0.5 * w_in[:, o_sg:o_sg + S5_W]], axis=1).astype(BF16)
    nb = RG_W // LANES
    per = rg_w_a.shape[0] // nb
    wg = jnp.stack([jnp.concatenate([_block_diag(rg_w_a[j * per:(j + 1) * per]),
                                     _block_diag(rg_w_x[j * per:(j + 1) * per])], axis=1)
                    for j in range(nb)]).astype(BF16)
    c_rg = row1(-RG_C * jax.nn.softplus(-rg_lambda))
    bm, lam_re, lam_im, cm = _s5_operators(s5_a_re, s5_a_im, s5_b_re, s5_b_im, s5_c_re, s5_c_im, s5_log_dt)
    n_state = 2 * S5_BLK_STATE * (S5_W // LANES)
    rg_scratch = [pltpu.VMEM((HALO + ROWS, RG_W), F32), pltpu.VMEM((ROWS, RG_W), F32),
                  pltpu.VMEM((ROWS, RG_W), F32), pltpu.VMEM((SUBLANES, RG_W), F32),
                  pltpu.VMEM((ROWS, n_state), F32), pltpu.VMEM((SUBLANES, n_state), F32)]
    y_spec, y_shape = _rows(RG_W + S5_W), jax.ShapeDtypeStruct((n_rows, RG_W + S5_W), BF16)
    if first:
        rg_scratch.append(pltpu.VMEM((d_model // LANES, ROWS, LANES), F32))
        y_spec, y_shape = [y_spec, _rows(d_model)], [y_shape, jax.ShapeDtypeStruct((n_rows, d_model), F32)]
    y_a = pl.pallas_call(
        functools.partial(_rg_s5_kernel, first=first),
        grid=grid,
        in_specs=[_seq_block(d_model) if first else _rows(d_model), _full((1, d_model)), _full(w_a.shape),
                  _full((CONV_K, RG_W)),
                  _full((1, RG_W)), _full(wg.shape), _full((1, RG_W)), _full((1, RG_W)), _full((1, RG_W)),
                  _full(bm.shape), _full(lam_re.shape), _full(lam_im.shape), _full(cm.shape),
                  _full((1, S5_W)), _full((S5_W, S5_W)), _full((1, S5_W))],
        out_specs=y_spec,
        out_shape=y_shape,
        scratch_shapes=rg_scratch,
        compiler_params=_seq_params(),
        name="rg_s5",
    )(h, row1(norm_g), w_a, rg_conv_w.astype(F32), row1(rg_conv_b), wg, row1(rg_b_a), row1(rg_b_x), c_rg,
      bm, lam_re, lam_im, cm, row1(s5_d), s5_w_glu.astype(BF16), row1(s5_b_glu))
    if first:
        y_a, h = y_a

    wqkv = w_in[:, o_q:o_q + 3 * GDN_W].astype(BF16)
    wz = (0.5 * w_in[:, o_z:o_z + GDN_W]).astype(BF16)
    zpad = jnp.zeros((d_model, LANES - GDN_H), w_in.dtype)
    wab = jnp.concatenate([w_in[:, o_a:o_a + GDN_H], zpad, w_in[:, o_b:o_b + GDN_H], zpad], axis=1).astype(BF16)
    lane_pad = lambda v: jnp.pad(v.reshape(1, -1).astype(F32), ((0, 0), (0, LANES - GDN_H)))
    sel = (jnp.arange(GDN_W)[:, None] // GDN_D == jnp.arange(LANES)[None, :]).astype(BF16)
    y_g = pl.pallas_call(
        _gdn_kernel,
        grid=grid,
        in_specs=[_rows(d_model), _full((1, d_model)), _full(wqkv.shape), _full(wz.shape), _full(wab.shape),
                  _full((CONV_K, 3 * GDN_W)), _full((1, LANES)), _full((1, LANES)), _full((1, GDN_D)),
                  _full(sel.shape)],
        out_specs=_rows(GDN_W),
        out_shape=jax.ShapeDtypeStruct((n_rows, GDN_W), BF16),
        scratch_shapes=[pltpu.VMEM((HALO + ROWS, 3 * GDN_W), F32), pltpu.VMEM((ROWS, d_model), BF16),
                        pltpu.VMEM((3 * GDN_H, ROWS, GDN_D), F32),
                        pltpu.VMEM((ROWS, 2 * GDN_W), BF16),
                        pltpu.VMEM((N_SC, ROWS, LANES), F32),
                        pltpu.VMEM((GDN_H, ROWS, GDN_D), F32),
                        pltpu.VMEM((SUBLANES * GDN_H, GDN_D, GDN_D), F32)],
        compiler_params=_seq_params(),
        name="gdn",
    )(h, row1(norm_g), wqkv, wz, wab, 0.5 * gdn_conv_w.astype(F32), lane_pad(-jnp.exp(gdn_a_log)),
      lane_pad(gdn_dt_bias), row1(gdn_norm_g), sel)

    wo_a = jnp.concatenate([w_out[0:RG_W], w_out[RG_W + GDN_W:]], axis=0).astype(BF16)
    wo_g = w_out[RG_W:RG_W + GDN_W].astype(BF16)
    d_ple = p_i.shape[-1]
    out_scratch = [pltpu.VMEM((d_ple // LANES, ROWS, LANES), F32)]
    if final:
        batch = p_i.shape[0]
        out_scratch.append(pltpu.VMEM((d_model // LANES, ROWS, LANES), F32))
        o_spec, o_shape = _seq_block(d_model), jax.ShapeDtypeStruct((batch, n_rows // batch, d_model), F32)
    else:
        o_spec, o_shape = _rows(d_model), jax.ShapeDtypeStruct((n_rows, d_model), F32)
    return pl.pallas_call(
        functools.partial(_out_kernel, final=final),
        grid=grid,
        in_specs=[_rows(d_model), _rows(RG_W + S5_W), _rows(GDN_W), _seq_block(d_ple), _full(wo_a.shape),
                  _full(wo_g.shape), _full((1, d_model)), _full((d_model, d_model)), _full((d_ple, d_model)),
                  _full((1, d_model))],
        out_specs=o_spec,
        out_shape=o_shape,
        scratch_shapes=out_scratch,
        compiler_params=pltpu.CompilerParams(dimension_semantics=("parallel",), vmem_limit_bytes=VMEM_LIMIT),
        name="out_ple",
    )(h, y_a, y_g, p_i, wo_a, wo_g, row1(ple_norm_g), ple_w_gate.astype(BF16), ple_w_proj.astype(BF16),
      row1(final_g))


def kernel(x, p, norm_g, w_in, rg_conv_w, rg_conv_b, rg_w_a, rg_b_a, rg_w_x, rg_b_x, rg_lambda, gdn_conv_w, gdn_a_log, gdn_dt_bias, gdn_norm_g, s5_a_re, s5_a_im, s5_b_re, s5_b_im, s5_c_re, s5_c_im, s5_d, s5_log_dt, s5_w_glu, s5_b_glu, w_out, ple_norm_g, ple_w_gate, ple_w_proj, final_norm_g):
    batch, seq, d_model = x.shape
    depth = p.shape[0]
    assert batch == SUBLANES and seq % TILE_T == 0 and depth >= 1
    assert x.dtype == F32
    layer_params = (norm_g, w_in, rg_conv_w, rg_conv_b, rg_w_a, rg_b_a, rg_w_x, rg_b_x, rg_lambda,
                    gdn_conv_w, gdn_a_log, gdn_dt_bias, gdn_norm_g, s5_a_re, s5_a_im, s5_b_re, s5_b_im,
                    s5_c_re, s5_c_im, s5_d, s5_log_dt, s5_w_glu, s5_b_glu, w_out, ple_norm_g, ple_w_gate,
                    ple_w_proj)
    h = x
    for i in range(depth):
        h = _layer(h, p[i], tuple(a[i] for a in layer_params), i == 0, i == depth - 1, final_norm_g)
    return h
```

```python
import functools

import jax
import jax.numpy as jnp
from jax import lax
from jax.experimental import pallas as pl
from jax.experimental.pallas import tpu as pltpu

F32 = jnp.float32
BF16 = jnp.bfloat16

NORM_EPS = 1e-6
RG_C = 8.0
CONV_K = 4

SUBLANES = 8
LANES = 128
TILE_T = 64
ROWS = TILE_T * SUBLANES
HALO = (CONV_K - 1) * SUBLANES

RG_W = 512
S5_W = 512
S5_GROUP = 16
S5_STATE = 64
S5_LANE_GROUPS = LANES // S5_GROUP
S5_BLK_STATE = S5_LANE_GROUPS * S5_STATE
GDN_H = 8
GDN_D = 128
GDN_W = GDN_H * GDN_D

VMEM_LIMIT = 56 * 1024 * 1024


def _rms(x, g):
    ms = jnp.mean(x * x, axis=-1, keepdims=True)
    return x * lax.rsqrt(ms + NORM_EPS) * g


def _sigmoid(x):
    return 1.0 / (1.0 + jnp.exp(-x))


def _silu(x):
    return x * _sigmoid(x)


def _silu_half(hx):
    return hx + hx * jnp.tanh(hx)


def _softplus(x):
    return jnp.maximum(x, 0.0) + jnp.log1p(jnp.exp(-jnp.abs(x)))


def _expm1(x):
    u = jnp.exp(x)
    um1 = u - 1.0
    regular = jnp.logical_and(um1 != 0.0, um1 != -1.0)
    v = um1 * x / jnp.log(jnp.where(regular, u, 2.0))
    return jnp.where(um1 == 0.0, x, jnp.where(um1 == -1.0, -1.0, v))


def _dot(a, b):
    return jnp.dot(a.astype(BF16), b.astype(BF16), preferred_element_type=F32)


def _dot_nt(a, b):
    return lax.dot_general(a.astype(BF16), b.astype(BF16), (((1,), (1,)), ((), ())),
                           preferred_element_type=F32)


def _dot_tn(a, b):
    return lax.dot_general(a.astype(BF16), b.astype(BF16), (((0,), (0,)), ((), ())),
                           preferred_element_type=F32)


def _causal_conv(x, buf, cw):
    buf[HALO:HALO + ROWS, :] = x
    y = cw[CONV_K - 1:CONV_K, :] * x
    for k in range(CONV_K - 1):
        off = HALO - (CONV_K - 1 - k) * SUBLANES
        y = y + cw[k:k + 1, :] * buf[off:off + ROWS, :]
    buf[0:HALO, :] = buf[ROWS:ROWS + HALO, :]
    return y


def _to_time_major(src_ref, chunk_buf):
    n_chunks = src_ref.shape[-1] // LANES
    for b in range(SUBLANES):
        for c in range(n_chunks):
            chunk_buf[c, pl.ds(b, TILE_T, stride=SUBLANES), :] = src_ref[b, :, c * LANES:(c + 1) * LANES]
    return jnp.concatenate([chunk_buf[c] for c in range(n_chunks)], axis=1)


def _from_time_major(val, chunk_buf, dst_ref):
    n_chunks = val.shape[-1] // LANES
    for c in range(n_chunks):
        chunk_buf[c] = val[:, c * LANES:(c + 1) * LANES]
    for b in range(SUBLANES):
        for c in range(n_chunks):
            dst_ref[b, :, c * LANES:(c + 1) * LANES] = chunk_buf[c, pl.ds(b, TILE_T, stride=SUBLANES), :]


def _rg_s5_out_kernel(h_ref, ng_ref, w_ref, cw_ref, cb_ref, wg_ref, ba_ref, bx_ref, c_ref,
                      bm_ref, lr_ref, li_ref, cm_ref, d_ref, wglu_ref, bglu_ref,
                      yg_ref, p_ref, wo_rg_ref, wo_s5_ref, wo_g_ref, png_ref, wgate_ref, wproj_ref, fg_ref,
                      o_ref,
                      xbuf, a_buf, i_buf, hcar, st_buf, scar, p_buf, *rest, final):
    @pl.when(pl.program_id(0) == 0)
    def _():
        xbuf[0:HALO, :] = jnp.zeros((HALO, RG_W), F32)
        hcar[...] = jnp.zeros_like(hcar)
        scar[...] = jnp.zeros_like(scar)

    h = h_ref[...]
    hn = _rms(h, ng_ref[...]).astype(BF16)
    proj = jnp.dot(hn, w_ref[...], preferred_element_type=F32)
    rg_x = proj[:, 0:RG_W]
    rg_gate = proj[:, RG_W:2 * RG_W]
    s5_u = proj[:, 2 * RG_W:2 * RG_W + S5_W]
    s5_gate = proj[:, 2 * RG_W + S5_W:]
    acc = h + jnp.dot(yg_ref[...], wo_g_ref[...], preferred_element_type=F32)

    xr = _causal_conv(rg_x, xbuf, cw_ref[...]) + cb_ref[...]
    pre_a, pre_x = [], []
    for j in range(RG_W // LANES):
        g = _dot(xr[:, j * LANES:(j + 1) * LANES], wg_ref[j])
        pre_a.append(g[:, :LANES])
        pre_x.append(g[:, LANES:])
    r = _sigmoid(jnp.concatenate(pre_a, axis=1) + ba_ref[...])
    gi = _sigmoid(jnp.concatenate(pre_x, axis=1) + bx_ref[...])
    log_a = c_ref[...] * r
    a_buf[...] = jnp.exp(log_a)
    i_buf[...] = jnp.sqrt(-_expm1(2.0 * log_a)) * (gi * xr)

    def rg_step(t, hc):
        r0 = pl.multiple_of(t * SUBLANES, SUBLANES)
        hc = a_buf[pl.ds(r0, SUBLANES), :] * hc + i_buf[pl.ds(r0, SUBLANES), :]
        i_buf[pl.ds(r0, SUBLANES), :] = hc
        return hc

    hcar[...] = lax.fori_loop(0, TILE_T, rg_step, hcar[...], unroll=True)
    y_rg = (i_buf[...] * _silu_half(rg_gate)).astype(BF16)
    acc = acc + jnp.dot(y_rg, wo_rg_ref[...], preferred_element_type=F32)

    nblk = S5_W // LANES
    sw = 2 * S5_BLK_STATE
    for j in range(nblk):
        st_buf[:, j * sw:(j + 1) * sw] = _dot(s5_u[:, j * LANES:(j + 1) * LANES], bm_ref[j])
    for j in range(nblk):
        c_re = j * sw
        c_im = j * sw + S5_BLK_STATE
        lr = jnp.broadcast_to(lr_ref[:, j * S5_BLK_STATE:(j + 1) * S5_BLK_STATE], (SUBLANES, S5_BLK_STATE))
        li = jnp.broadcast_to(li_ref[:, j * S5_BLK_STATE:(j + 1) * S5_BLK_STATE], (SUBLANES, S5_BLK_STATE))

        def s5_step(t, carry, c_re=c_re, c_im=c_im, lr=lr, li=li):
            sr, si = carry
            r0 = pl.multiple_of(t * SUBLANES, SUBLANES)
            nr = lr * sr - li * si + st_buf[pl.ds(r0, SUBLANES), c_re:c_re + S5_BLK_STATE]
            ni = lr * si + li * sr + st_buf[pl.ds(r0, SUBLANES), c_im:c_im + S5_BLK_STATE]
            st_buf[pl.ds(r0, SUBLANES), c_re:c_re + S5_BLK_STATE] = nr
            st_buf[pl.ds(r0, SUBLANES), c_im:c_im + S5_BLK_STATE] = ni
            return nr, ni

        sr, si = lax.fori_loop(
            0, TILE_T, s5_step,
            (scar[:, c_re:c_re + S5_BLK_STATE], scar[:, c_im:c_im + S5_BLK_STATE]), unroll=True)
        scar[:, c_re:c_re + S5_BLK_STATE] = sr
        scar[:, c_im:c_im + S5_BLK_STATE] = si
    ys = [_dot(st_buf[:, j * sw:(j + 1) * sw], cm_ref[j]) for j in range(nblk)]
    y5 = jnp.concatenate(ys, axis=1) + d_ref[...] * s5_u
    z5 = jax.nn.gelu(y5)
    y5 = z5 * _sigmoid(_dot(z5, wglu_ref[...]) + bglu_ref[...])
    y_s5 = (y5 * _silu_half(s5_gate)).astype(BF16)

    h = acc + jnp.dot(y_s5, wo_s5_ref[...], preferred_element_type=F32)
    gate = _sigmoid(_dot(_rms(h, png_ref[...]), wgate_ref[...]))
    h = h + gate * _dot(_to_time_major(p_ref, p_buf), wproj_ref[...])
    if final:
        _from_time_major(_rms(h, fg_ref[...]), rest[0], o_ref)
    else:
        o_ref[...] = h


N_SC = 8
SC_GC, SC_BRK, SC_RQ, SC_QEG, SC_KDL, SC_RK, SC_BETA, SC_TWC = range(N_SC)
SC_COLS = (SC_GC, SC_BRK, SC_RQ, SC_QEG, SC_KDL)
SC_ROWS = (SC_GC, SC_RK, SC_BETA, SC_TWC)
CONV_RB = 32
GDN_PAR = 4


def _gdn_kernel(h_ref, ng_ref, wqkv_ref, wz_ref, wab_ref, cw_ref, nea_ref, dtb_ref, gn_ref, sel_ref,
                y_ref, *rest, first):
    if first:
        htm_ref, cbuf, hn_buf, qkv_s, sq_s, sc_s, o_s, st_ref, chunk_buf = rest
    else:
        cbuf, hn_buf, qkv_s, sq_s, sc_s, o_s, st_ref = rest

    @pl.when(pl.program_id(0) == 0)
    def _():
        cbuf[0:HALO, :] = jnp.zeros((HALO, 3 * GDN_W), F32)
        st_ref[...] = jnp.zeros_like(st_ref)

    if first:
        h = _to_time_major(h_ref, chunk_buf)
        htm_ref[...] = h
    else:
        h = h_ref[...]
    hn_buf[...] = _rms(h, ng_ref[...]).astype(BF16)

    ab = jnp.dot(hn_buf[...], wab_ref[...], preferred_element_type=F32)
    g = nea_ref[...] * _softplus(ab[:, :LANES] + dtb_ref[...])
    beta = _sigmoid(ab[:, LANES:])
    gc = g
    d = 1
    while d < TILE_T:
        sh = d * SUBLANES
        gc = gc + jnp.concatenate([jnp.zeros((sh, LANES), F32), gc[:ROWS - sh, :]], axis=0)
        d *= 2
    g_last = jnp.broadcast_to(gc[ROWS - SUBLANES:, :][None], (TILE_T, SUBLANES, LANES)).reshape(ROWS, LANES)

    for c in range(3):
        cbuf[HALO:HALO + ROWS, c * GDN_W:(c + 1) * GDN_W] = jnp.dot(
            hn_buf[...], wqkv_ref[:, c * GDN_W:(c + 1) * GDN_W], preferred_element_type=F32)

    def conv_rows(i, carry):
        r0 = pl.multiple_of(i * CONV_RB, CONV_RB)
        for c in range(3 * GDN_H):
            cols = slice(c * GDN_D, (c + 1) * GDN_D)
            y = cw_ref[CONV_K - 1:CONV_K, cols] * cbuf[pl.ds(r0 + HALO, CONV_RB), cols]
            for k in range(CONV_K - 1):
                off = HALO - (CONV_K - 1 - k) * SUBLANES
                y = y + cw_ref[k:k + 1, cols] * cbuf[pl.ds(r0 + off, CONV_RB), cols]
            act = _silu_half(y)
            qkv_s[c, pl.ds(r0, CONV_RB), :] = act
            if c < 2 * GDN_H:
                sq_s[pl.ds(r0, CONV_RB), cols] = (act * act).astype(BF16)
        return carry

    lax.fori_loop(0, ROWS // CONV_RB, conv_rows, 0)
    cbuf[0:HALO, :] = cbuf[ROWS:ROWS + HALO, :]

    ss_q = jnp.dot(sq_s[:, 0:GDN_W], sel_ref[...], preferred_element_type=F32)
    ss_k = jnp.dot(sq_s[:, GDN_W:2 * GDN_W], sel_ref[...], preferred_element_type=F32)
    rq = lax.rsqrt(ss_q + NORM_EPS) * (GDN_D ** -0.5)
    rk = lax.rsqrt(ss_k + NORM_EPS)
    egc = jnp.exp(gc)
    sc_s[SC_GC] = gc
    sc_s[SC_BRK] = beta * rk
    sc_s[SC_RQ] = rq
    sc_s[SC_QEG] = rq * egc
    sc_s[SC_KDL] = rk * jnp.exp(g_last - gc)
    sc_s[SC_RK] = rk
    sc_s[SC_BETA] = beta
    sc_s[SC_TWC] = beta * rk * egc

    row = lax.broadcasted_iota(jnp.int32, (TILE_T, TILE_T), 0)
    col = lax.broadcasted_iota(jnp.int32, (TILE_T, TILE_T), 1)
    causal = row >= col
    strict = row > col
    eye = (row == col).astype(F32)

    def per_group(i, carry):
        chains = [(i * GDN_PAR + s, h) for s in range(GDN_PAR) for h in range(GDN_H)]
        colv, rowv = [], []
        for s in range(GDN_PAR):
            rows = pl.ds(i * GDN_PAR + s, TILE_T, stride=SUBLANES)
            colv.append({n: sc_s[n, rows, :] for n in SC_COLS})
            pad = jnp.zeros((TILE_T, LANES), F32)
            rowv.append({n: jnp.concatenate([sc_s[n, rows, :], pad], axis=0).T for n in SC_ROWS})
        ck = lambda s, n, h: colv[s][n][:, h:h + 1]
        rw = lambda s, n, h: rowv[s][n][h:h + 1, 0:TILE_T]

        q_b, k_b, v_b = [], [], []
        for b, h in chains:
            rows = pl.ds(b, TILE_T, stride=SUBLANES)
            q_b.append(qkv_s[h, rows, :].astype(BF16))
            k_b.append(qkv_s[GDN_H + h, rows, :].astype(BF16))
            v_b.append(qkv_s[2 * GDN_H + h, rows, :].astype(BF16))
        n_ch = len(chains)
        kk = [_dot_nt(jnp.concatenate([k_b[c], q_b[c]], axis=0), k_b[c]) for c in range(n_ch)]
        a_qk, neg = [], []
        for c, (b, h) in enumerate(chains):
            s = c // GDN_H
            diff = ck(s, SC_GC, h) - rw(s, SC_GC, h)
            dk = jnp.where(causal, jnp.exp(jnp.where(causal, diff, 0.0)), 0.0) * rw(s, SC_RK, h)
            neg.append(jnp.where(strict, -(kk[c][:TILE_T] * ck(s, SC_BRK, h)) * dk, 0.0))
            a_qk.append(kk[c][TILE_T:] * ck(s, SC_RQ, h) * dk)
        tinv = [eye + n for n in neg]
        pw = [_dot(n, n) for n in neg]
        span = 2
        while span < TILE_T:
            tinv = [t + _dot(t, p) for t, p in zip(tinv, pw)]
            span *= 2
            if span < TILE_T:
                pw = [_dot(p, p) for p in pw]
        u, w = [], []
        for c, (b, h) in enumerate(chains):
            s = c // GDN_H
            u.append(_dot(tinv[c] * rw(s, SC_BETA, h), v_b[c]))
            w.append(_dot(tinv[c] * rw(s, SC_TWC, h), k_b[c]))
        state = [st_ref[b * GDN_H + h] for b, h in chains]
        ws_qs = [_dot(jnp.concatenate([w[c].astype(BF16), q_b[c]], axis=0), state[c]) for c in range(n_ch)]
        v_new = [u[c] - ws_qs[c][:TILE_T] for c in range(n_ch)]
        for c, (b, h) in enumerate(chains):
            s = c // GDN_H
            o_s[h, pl.ds(b, TILE_T, stride=SUBLANES), :] = (
                ws_qs[c][TILE_T:] * ck(s, SC_QEG, h) + _dot(a_qk[c], v_new[c]))
        for c, (b, h) in enumerate(chains):
            s = c // GDN_H
            e_last = jnp.exp(rowv[s][SC_GC][h:h + 1, TILE_T - 1:TILE_T])
            st_ref[b * GDN_H + h] = state[c] * e_last + _dot_tn(k_b[c], v_new[c] * ck(s, SC_KDL, h))
        return carry

    lax.fori_loop(0, SUBLANES // GDN_PAR, per_group, 0)

    z = jnp.dot(hn_buf[...], wz_ref[...], preferred_element_type=F32)
    for h in range(GDN_H):
        sl = slice(h * GDN_D, (h + 1) * GDN_D)
        y_ref[:, sl] = (_rms(o_s[h], gn_ref[...]) * _silu_half(z[:, sl])).astype(y_ref.dtype)


def _full(shape):
    nd = len(shape)
    return pl.BlockSpec(shape, lambda i: (0,) * nd, pipeline_mode=pl.Buffered(1))


def _rows(width):
    return pl.BlockSpec((ROWS, width), lambda i: (i, 0))


def _seq_params():
    return pltpu.CompilerParams(dimension_semantics=("arbitrary",), vmem_limit_bytes=VMEM_LIMIT)


def _block_diag(blocks):
    n, r, c = blocks.shape
    eye = jnp.eye(n, dtype=blocks.dtype)
    return jnp.einsum('grc,gh->grhc', blocks, eye).reshape(n * r, n * c)


def _s5_operators(a_re, a_im, b_re, b_im, c_re, c_im, log_dt):
    lam = lax.complex(a_re, a_im)
    dt = jnp.exp(log_dt)[:, None]
    lam_bar = jnp.exp(lam * dt)
    b_bar = ((lam_bar - 1.0) / lam)[..., None] * lax.complex(b_re, b_im)
    nblk = S5_W // LANES
    bt = jnp.swapaxes(b_bar, 1, 2).reshape(nblk, S5_LANE_GROUPS, S5_GROUP, S5_STATE)
    bm = jnp.concatenate([jax.vmap(_block_diag)(jnp.real(bt)), jax.vmap(_block_diag)(jnp.imag(bt))], axis=-1)
    ct = jnp.swapaxes(lax.complex(c_re, c_im), 1, 2).reshape(nblk, S5_LANE_GROUPS, S5_STATE, S5_GROUP)
    cm = jnp.concatenate([jax.vmap(_block_diag)(jnp.real(ct)), -jax.vmap(_block_diag)(jnp.imag(ct))], axis=1)
    return (bm.astype(BF16), jnp.real(lam_bar).reshape(1, -1), jnp.imag(lam_bar).reshape(1, -1),
            cm.astype(BF16))


def _seq_block(width):
    return pl.BlockSpec((SUBLANES, TILE_T, width), lambda i: (0, i, 0))


def _layer(h, p_i, prm, first, final, final_g):
    d_model = h.shape[-1]
    n_rows = h.shape[0] * h.shape[1] if first else h.shape[0]
    grid = (n_rows // ROWS,)
    (norm_g, w_in, rg_conv_w, rg_conv_b, rg_w_a, rg_b_a, rg_w_x, rg_b_x, rg_lambda,
     gdn_conv_w, gdn_a_log, gdn_dt_bias, gdn_norm_g, s5_a_re, s5_a_im, s5_b_re, s5_b_im,
     s5_c_re, s5_c_im, s5_d, s5_log_dt, s5_w_glu, s5_b_glu, w_out, ple_norm_g, ple_w_gate,
     ple_w_proj) = prm

    o_rgx, o_rgg = 0, RG_W
    o_q = 2 * RG_W
    o_z = o_q + 3 * GDN_W
    o_b = o_z + GDN_W
    o_a = o_b + GDN_H
    o_u = o_a + GDN_H
    o_sg = o_u + S5_W
    row1 = lambda v: v.reshape(1, -1).astype(F32)

    wqkv = w_in[:, o_q:o_q + 3 * GDN_W].astype(BF16)
    wz = (0.5 * w_in[:, o_z:o_z + GDN_W]).astype(BF16)
    zpad = jnp.zeros((d_model, LANES - GDN_H), w_in.dtype)
    wab = jnp.concatenate([w_in[:, o_a:o_a + GDN_H], zpad, w_in[:, o_b:o_b + GDN_H], zpad], axis=1).astype(BF16)
    lane_pad = lambda v: jnp.pad(v.reshape(1, -1).astype(F32), ((0, 0), (0, LANES - GDN_H)))
    sel = (jnp.arange(GDN_W)[:, None] // GDN_D == jnp.arange(LANES)[None, :]).astype(BF16)
    gdn_scratch = [pltpu.VMEM((HALO + ROWS, 3 * GDN_W), F32), pltpu.VMEM((ROWS, d_model), BF16),
                   pltpu.VMEM((3 * GDN_H, ROWS, GDN_D), F32),
                   pltpu.VMEM((ROWS, 2 * GDN_W), BF16),
                   pltpu.VMEM((N_SC, ROWS, LANES), F32),
                   pltpu.VMEM((GDN_H, ROWS, GDN_D), F32),
                   pltpu.VMEM((SUBLANES * GDN_H, GDN_D, GDN_D), F32)]
    y_spec, y_shape = _rows(GDN_W), jax.ShapeDtypeStruct((n_rows, GDN_W), BF16)
    if first:
        gdn_scratch.append(pltpu.VMEM((d_model // LANES, ROWS, LANES), F32))
        y_spec, y_shape = [y_spec, _rows(d_model)], [y_shape, jax.ShapeDtypeStruct((n_rows, d_model), F32)]
    y_g = pl.pallas_call(
        functools.partial(_gdn_kernel, first=first),
        grid=grid,
        in_specs=[_seq_block(d_model) if first else _rows(d_model), _full((1, d_model)), _full(wqkv.shape),
                  _full(wz.shape), _full(wab.shape), _full((CONV_K, 3 * GDN_W)), _full((1, LANES)),
                  _full((1, LANES)), _full((1, GDN_D)), _full(sel.shape)],
        out_specs=y_spec,
        out_shape=y_shape,
        scratch_shapes=gdn_scratch,
        compiler_params=_seq_params(),
        name="gdn",
    )(h, row1(norm_g), wqkv, wz, wab, 0.5 * gdn_conv_w.astype(F32), lane_pad(-jnp.exp(gdn_a_log)),
      lane_pad(gdn_dt_bias), row1(gdn_norm_g), sel)
    if first:
        y_g, h = y_g

    w_a = jnp.concatenate([w_in[:, o_rgx:o_rgx + RG_W], 0.5 * w_in[:, o_rgg:o_rgg + RG_W],
                           w_in[:, o_u:o_u + S5_W], 0.5 * w_in[:, o_sg:o_sg + S5_W]], axis=1).astype(BF16)
    nb = RG_W // LANES
    per = rg_w_a.shape[0] // nb
    wg = jnp.stack([jnp.concatenate([_block_diag(rg_w_a[j * per:(j + 1) * per]),
                                     _block_diag(rg_w_x[j * per:(j + 1) * per])], axis=1)
                    for j in range(nb)]).astype(BF16)
    c_rg = row1(-RG_C * jax.nn.softplus(-rg_lambda))
    bm, lam_re, lam_im, cm = _s5_operators(s5_a_re, s5_a_im, s5_b_re, s5_b_im, s5_c_re, s5_c_im, s5_log_dt)
    n_state = 2 * S5_BLK_STATE * (S5_W // LANES)
    wo_rg = w_out[0:RG_W].astype(BF16)
    wo_g = w_out[RG_W:RG_W + GDN_W].astype(BF16)
    wo_s5 = w_out[RG_W + GDN_W:].astype(BF16)
    d_ple = p_i.shape[-1]
    scratch = [pltpu.VMEM((HALO + ROWS, RG_W), F32), pltpu.VMEM((ROWS, RG_W), F32),
               pltpu.VMEM((ROWS, RG_W), F32), pltpu.VMEM((SUBLANES, RG_W), F32),
               pltpu.VMEM((ROWS, n_state), F32), pltpu.VMEM((SUBLANES, n_state), F32),
               pltpu.VMEM((d_ple // LANES, ROWS, LANES), F32)]
    if final:
        batch = p_i.shape[0]
        scratch.append(pltpu.VMEM((d_model // LANES, ROWS, LANES), F32))
        o_spec, o_shape = _seq_block(d_model), jax.ShapeDtypeStruct((batch, n_rows // batch, d_model), F32)
    else:
        o_spec, o_shape = _rows(d_model), jax.ShapeDtypeStruct((n_rows, d_model), F32)
    return pl.pallas_call(
        functools.partial(_rg_s5_out_kernel, final=final),
        grid=grid,
        in_specs=[_rows(d_model), _full((1, d_model)), _full(w_a.shape), _full((CONV_K, RG_W)),
                  _full((1, RG_W)), _full(wg.shape), _full((1, RG_W)), _full((1, RG_W)), _full((1, RG_W)),
                  _full(bm.shape), _full(lam_re.shape), _full(lam_im.shape), _full(cm.shape),
                  _full((1, S5_W)), _full((S5_W, S5_W)), _full((1, S5_W)),
                  _rows(GDN_W), _seq_block(d_ple), _full(wo_rg.shape), _full(wo_s5.shape), _full(wo_g.shape),
                  _full((1, d_model)), _full((d_model, d_model)), _full((d_ple, d_model)), _full((1, d_model))],
        out_specs=o_spec,
        out_shape=o_shape,
        scratch_shapes=scratch,
        compiler_params=_seq_params(),
        name="rg_s5_out",
    )(h, row1(norm_g), w_a, rg_conv_w.astype(F32), row1(rg_conv_b), wg, row1(rg_b_a), row1(rg_b_x), c_rg,
      bm, lam_re, lam_im, cm, row1(s5_d), s5_w_glu.astype(BF16), row1(s5_b_glu),
      y_g, p_i, wo_rg, wo_s5, wo_g, row1(ple_norm_g), ple_w_gate.astype(BF16), ple_w_proj.astype(BF16),
      row1(final_g))


def kernel(x, p, norm_g, w_in, rg_conv_w, rg_conv_b, rg_w_a, rg_b_a, rg_w_x, rg_b_x, rg_lambda, gdn_conv_w, gdn_a_log, gdn_dt_bias, gdn_norm_g, s5_a_re, s5_a_im, s5_b_re, s5_b_im, s5_c_re, s5_c_im, s5_d, s5_log_dt, s5_w_glu, s5_b_glu, w_out, ple_norm_g, ple_w_gate, ple_w_proj, final_norm_g):
    batch, seq, d_model = x.shape
    depth = p.shape[0]
    assert batch == SUBLANES and seq % TILE_T == 0 and depth >= 1
    assert x.dtype == F32
    layer_params = (norm_g, w_in, rg_conv_w, rg_conv_b, rg_w_a, rg_b_a, rg_w_x, rg_b_x, rg_lambda,
                    gdn_conv_w, gdn_a_log, gdn_dt_bias, gdn_norm_g, s5_a_re, s5_a_im, s5_b_re, s5_b_im,
                    s5_c_re, s5_c_im, s5_d, s5_log_dt, s5_w_glu, s5_b_glu, w_out, ple_norm_g, ple_w_gate,
                    ple_w_proj)
    h = x
    for i in range(depth):
        h = _layer(h, p[i], tuple(a[i] for a in layer_params), i == 0, i == depth - 1, final_norm_g)
    return h
```

```python
import functools

import jax
import jax.numpy as jnp
from jax import lax
from jax.experimental import pallas as pl
from jax.experimental.pallas import tpu as pltpu

F32 = jnp.float32
BF16 = jnp.bfloat16

NORM_EPS = 1e-6
RG_C = 8.0
CONV_K = 4

SUBLANES = 8
LANES = 128
TILE_T = 64
ROWS = TILE_T * SUBLANES
HALO = (CONV_K - 1) * SUBLANES

RG_W = 512
S5_W = 512
S5_GROUP = 16
S5_STATE = 64
S5_LANE_GROUPS = LANES // S5_GROUP
S5_BLK_STATE = S5_LANE_GROUPS * S5_STATE
GDN_H = 8
GDN_D = 128
GDN_W = GDN_H * GDN_D

VMEM_LIMIT = 56 * 1024 * 1024


def _rms(x, g):
    ms = jnp.mean(x * x, axis=-1, keepdims=True)
    return x * lax.rsqrt(ms + NORM_EPS) * g


def _sigmoid(x):
    return 1.0 / (1.0 + jnp.exp(-x))


def _silu(x):
    return x * _sigmoid(x)


def _silu_half(hx):
    return hx + hx * jnp.tanh(hx)


def _softplus(x):
    return jnp.maximum(x, 0.0) + jnp.log1p(jnp.exp(-jnp.abs(x)))


def _expm1(x):
    u = jnp.exp(x)
    um1 = u - 1.0
    regular = jnp.logical_and(um1 != 0.0, um1 != -1.0)
    v = um1 * x / jnp.log(jnp.where(regular, u, 2.0))
    return jnp.where(um1 == 0.0, x, jnp.where(um1 == -1.0, -1.0, v))


def _dot(a, b):
    return jnp.dot(a.astype(BF16), b.astype(BF16), preferred_element_type=F32)


def _dot_nt(a, b):
    return lax.dot_general(a.astype(BF16), b.astype(BF16), (((1,), (1,)), ((), ())),
                           preferred_element_type=F32)


def _dot_tn(a, b):
    return lax.dot_general(a.astype(BF16), b.astype(BF16), (((0,), (0,)), ((), ())),
                           preferred_element_type=F32)


def _causal_conv(x, buf, cw):
    buf[HALO:HALO + ROWS, :] = x
    y = cw[CONV_K - 1:CONV_K, :] * x
    for k in range(CONV_K - 1):
        off = HALO - (CONV_K - 1 - k) * SUBLANES
        y = y + cw[k:k + 1, :] * buf[off:off + ROWS, :]
    buf[0:HALO, :] = buf[ROWS:ROWS + HALO, :]
    return y


def _to_time_major(src_ref, chunk_buf):
    n_chunks = src_ref.shape[-1] // LANES
    for b in range(SUBLANES):
        for c in range(n_chunks):
            chunk_buf[c, pl.ds(b, TILE_T, stride=SUBLANES), :] = src_ref[b, :, c * LANES:(c + 1) * LANES]
    return jnp.concatenate([chunk_buf[c] for c in range(n_chunks)], axis=1)


def _from_time_major(val, chunk_buf, dst_ref):
    n_chunks = val.shape[-1] // LANES
    for c in range(n_chunks):
        chunk_buf[c] = val[:, c * LANES:(c + 1) * LANES]
    for b in range(SUBLANES):
        for c in range(n_chunks):
            dst_ref[b, :, c * LANES:(c + 1) * LANES] = chunk_buf[c, pl.ds(b, TILE_T, stride=SUBLANES), :]


def _rg_s5_out_kernel(h_ref, ng_ref, w_ref, cw_ref, cb_ref, wg_ref, ba_ref, bx_ref, c_ref,
                      bm_ref, lr_ref, li_ref, cm_ref, d_ref, wglu_ref, bglu_ref,
                      yg_ref, p_ref, wo_rg_ref, wo_s5_ref, wo_g_ref, png_ref, wgate_ref, wproj_ref, fg_ref,
                      o_ref,
                      xbuf, a_buf, i_buf, hcar, st_buf, scar, p_buf, *rest, final):
    @pl.when(pl.program_id(0) == 0)
    def _():
        xbuf[0:HALO, :] = jnp.zeros((HALO, RG_W), F32)
        hcar[...] = jnp.zeros_like(hcar)
        scar[...] = jnp.zeros_like(scar)

    h = h_ref[...]
    hn = _rms(h, ng_ref[...]).astype(BF16)
    proj = jnp.dot(hn, w_ref[...], preferred_element_type=F32)
    rg_x = proj[:, 0:RG_W]
    rg_gate = proj[:, RG_W:2 * RG_W]
    s5_u = proj[:, 2 * RG_W:2 * RG_W + S5_W]
    s5_gate = proj[:, 2 * RG_W + S5_W:]
    acc = h + jnp.dot(yg_ref[...], wo_g_ref[...], preferred_element_type=F32)

    xr = _causal_conv(rg_x, xbuf, cw_ref[...]) + cb_ref[...]
    pre_a, pre_x = [], []
    for j in range(RG_W // LANES):
        g = _dot(xr[:, j * LANES:(j + 1) * LANES], wg_ref[j])
        pre_a.append(g[:, :LANES])
        pre_x.append(g[:, LANES:])
    r = _sigmoid(jnp.concatenate(pre_a, axis=1) + ba_ref[...])
    gi = _sigmoid(jnp.concatenate(pre_x, axis=1) + bx_ref[...])
    log_a = c_ref[...] * r
    a_buf[...] = jnp.exp(log_a)
    i_buf[...] = jnp.sqrt(-_expm1(2.0 * log_a)) * (gi * xr)

    def rg_step(t, hc):
        r0 = pl.multiple_of(t * SUBLANES, SUBLANES)
        hc = a_buf[pl.ds(r0, SUBLANES), :] * hc + i_buf[pl.ds(r0, SUBLANES), :]
        i_buf[pl.ds(r0, SUBLANES), :] = hc
        return hc

    hcar[...] = lax.fori_loop(0, TILE_T, rg_step, hcar[...], unroll=True)
    y_rg = (i_buf[...] * _silu_half(rg_gate)).astype(BF16)
    acc = acc + jnp.dot(y_rg, wo_rg_ref[...], preferred_element_type=F32)

    nblk = S5_W // LANES
    sw = 2 * S5_BLK_STATE
    for j in range(nblk):
        st_buf[:, j * sw:(j + 1) * sw] = _dot(s5_u[:, j * LANES:(j + 1) * LANES], bm_ref[j])
    for j in range(nblk):
        c_re = j * sw
        c_im = j * sw + S5_BLK_STATE
        lr = jnp.broadcast_to(lr_ref[:, j * S5_BLK_STATE:(j + 1) * S5_BLK_STATE], (SUBLANES, S5_BLK_STATE))
        li = jnp.broadcast_to(li_ref[:, j * S5_BLK_STATE:(j + 1) * S5_BLK_STATE], (SUBLANES, S5_BLK_STATE))

        def s5_step(t, carry, c_re=c_re, c_im=c_im, lr=lr, li=li):
            sr, si = carry
            r0 = pl.multiple_of(t * SUBLANES, SUBLANES)
            nr = lr * sr - li * si + st_buf[pl.ds(r0, SUBLANES), c_re:c_re + S5_BLK_STATE]
            ni = lr * si + li * sr + st_buf[pl.ds(r0, SUBLANES), c_im:c_im + S5_BLK_STATE]
            st_buf[pl.ds(r0, SUBLANES), c_re:c_re + S5_BLK_STATE] = nr
            st_buf[pl.ds(r0, SUBLANES), c_im:c_im + S5_BLK_STATE] = ni
            return nr, ni

        sr, si = lax.fori_loop(
            0, TILE_T, s5_step,
            (scar[:, c_re:c_re + S5_BLK_STATE], scar[:, c_im:c_im + S5_BLK_STATE]), unroll=True)
        scar[:, c_re:c_re + S5_BLK_STATE] = sr
        scar[:, c_im:c_im + S5_BLK_STATE] = si
    ys = [_dot(st_buf[:, j * sw:(j + 1) * sw], cm_ref[j]) for j in range(nblk)]
    y5 = jnp.concatenate(ys, axis=1) + d_ref[...] * s5_u
    z5 = jax.nn.gelu(y5)
    y5 = z5 * _sigmoid(_dot(z5, wglu_ref[...]) + bglu_ref[...])
    y_s5 = (y5 * _silu_half(s5_gate)).astype(BF16)

    h = acc + jnp.dot(y_s5, wo_s5_ref[...], preferred_element_type=F32)
    gate = _sigmoid(_dot(_rms(h, png_ref[...]), wgate_ref[...]))
    h = h + gate * _dot(_to_time_major(p_ref, p_buf), wproj_ref[...])
    if final:
        _from_time_major(_rms(h, fg_ref[...]), rest[0], o_ref)
    else:
        o_ref[...] = h


N_SC = 8
SC_GC, SC_BRK, SC_RQ, SC_QEG, SC_KDL, SC_RK, SC_BETA, SC_TWC = range(N_SC)
SC_COLS = (SC_GC, SC_BRK, SC_RQ, SC_QEG, SC_KDL)
SC_ROWS = (SC_GC, SC_RK, SC_BETA, SC_TWC)
CONV_RB = 32
GDN_PAR = 4


def _gdn_kernel(h_ref, ng_ref, wqkv_ref, wz_ref, wab_ref, cw_ref, nea_ref, dtb_ref, gn_ref, sel_ref,
                y_ref, *rest, first):
    if first:
        htm_ref, cbuf, hn_buf, qkv_s, sq_s, sc_s, o_s, st_ref, chunk_buf = rest
    else:
        cbuf, hn_buf, qkv_s, sq_s, sc_s, o_s, st_ref = rest

    @pl.when(pl.program_id(0) == 0)
    def _():
        cbuf[0:HALO, :] = jnp.zeros((HALO, 3 * GDN_W), F32)
        st_ref[...] = jnp.zeros_like(st_ref)

    if first:
        h = _to_time_major(h_ref, chunk_buf)
        htm_ref[...] = h
    else:
        h = h_ref[...]
    hn_buf[...] = _rms(h, ng_ref[...]).astype(BF16)

    ab = jnp.dot(hn_buf[...], wab_ref[...], preferred_element_type=F32)
    g = nea_ref[...] * _softplus(ab[:, :LANES] + dtb_ref[...])
    beta = _sigmoid(ab[:, LANES:])
    gc = g
    d = 1
    while d < TILE_T:
        sh = d * SUBLANES
        gc = gc + jnp.concatenate([jnp.zeros((sh, LANES), F32), gc[:ROWS - sh, :]], axis=0)
        d *= 2
    g_last = jnp.broadcast_to(gc[ROWS - SUBLANES:, :][None], (TILE_T, SUBLANES, LANES)).reshape(ROWS, LANES)

    def project(c3):
        cbuf[HALO:HALO + ROWS, c3 * GDN_W:(c3 + 1) * GDN_W] = jnp.dot(
            hn_buf[...], wqkv_ref[:, c3 * GDN_W:(c3 + 1) * GDN_W], preferred_element_type=F32)

    def conv_group(c3):
        for c in range(c3 * GDN_H, (c3 + 1) * GDN_H):
            cols = slice(c * GDN_D, (c + 1) * GDN_D)
            taps = [cw_ref[k:k + 1, cols] for k in range(CONV_K)]
            for r0 in range(0, ROWS, CONV_RB):
                y = taps[CONV_K - 1] * cbuf[r0 + HALO:r0 + HALO + CONV_RB, cols]
                for k in range(CONV_K - 1):
                    off = r0 + HALO - (CONV_K - 1 - k) * SUBLANES
                    y = y + taps[k] * cbuf[off:off + CONV_RB, cols]
                act = _silu_half(y)
                qkv_s[c, r0:r0 + CONV_RB, :] = act
                if c3 < 2:
                    sq_s[r0:r0 + CONV_RB, cols] = (act * act).astype(BF16)

    project(0)
    project(1)
    conv_group(0)
    project(2)
    conv_group(1)
    z = jnp.dot(hn_buf[...], wz_ref[...], preferred_element_type=F32)
    conv_group(2)
    halo = cbuf[ROWS:ROWS + HALO, :]
    cbuf[0:HALO, :] = halo

    ss_q = jnp.dot(sq_s[:, 0:GDN_W], sel_ref[...], preferred_element_type=F32)
    ss_k = jnp.dot(sq_s[:, GDN_W:2 * GDN_W], sel_ref[...], preferred_element_type=F32)
    rq = lax.rsqrt(ss_q + NORM_EPS) * (GDN_D ** -0.5)
    rk = lax.rsqrt(ss_k + NORM_EPS)
    egc = jnp.exp(gc)
    sc_s[SC_GC] = gc
    sc_s[SC_BRK] = beta * rk
    sc_s[SC_RQ] = rq
    sc_s[SC_QEG] = rq * egc
    sc_s[SC_KDL] = rk * jnp.exp(g_last - gc)
    sc_s[SC_RK] = rk
    sc_s[SC_BETA] = beta
    sc_s[SC_TWC] = beta * rk * egc

    row = lax.broadcasted_iota(jnp.int32, (TILE_T, TILE_T), 0)
    col = lax.broadcasted_iota(jnp.int32, (TILE_T, TILE_T), 1)
    causal = row >= col
    strict = row > col
    eye = (row == col).astype(F32)

    def per_group(i, carry):
        chains = [(i * GDN_PAR + s, h) for s in range(GDN_PAR) for h in range(GDN_H)]
        colv, rowv = [], []
        for s in range(GDN_PAR):
            rows = pl.ds(i * GDN_PAR + s, TILE_T, stride=SUBLANES)
            colv.append({n: sc_s[n, rows, :] for n in SC_COLS})
            pad = jnp.zeros((TILE_T, LANES), F32)
            rowv.append({n: jnp.concatenate([sc_s[n, rows, :], pad], axis=0).T for n in SC_ROWS})
        ck = lambda s, n, h: colv[s][n][:, h:h + 1]
        rw = lambda s, n, h: rowv[s][n][h:h + 1, 0:TILE_T]

        q_b, k_b, v_b = [], [], []
        for b, h in chains:
            rows = pl.ds(b, TILE_T, stride=SUBLANES)
            q_b.append(qkv_s[h, rows, :].astype(BF16))
            k_b.append(qkv_s[GDN_H + h, rows, :].astype(BF16))
            v_b.append(qkv_s[2 * GDN_H + h, rows, :].astype(BF16))
        n_ch = len(chains)
        kk = [_dot_nt(jnp.concatenate([k_b[c], q_b[c]], axis=0), k_b[c]) for c in range(n_ch)]
        a_qk, neg = [], []
        for c, (b, h) in enumerate(chains):
            s = c // GDN_H
            diff = ck(s, SC_GC, h) - rw(s, SC_GC, h)
            dk = jnp.where(causal, jnp.exp(jnp.where(causal, diff, 0.0)), 0.0) * rw(s, SC_RK, h)
            neg.append(jnp.where(strict, -(kk[c][:TILE_T] * ck(s, SC_BRK, h)) * dk, 0.0))
            a_qk.append(kk[c][TILE_T:] * ck(s, SC_RQ, h) * dk)
        tinv = [eye + n for n in neg]
        pw = [_dot(n, n) for n in neg]
        span = 2
        while span < TILE_T:
            tinv = [t + _dot(t, p) for t, p in zip(tinv, pw)]
            span *= 2
            if span < TILE_T:
                pw = [_dot(p, p) for p in pw]
        u, w = [], []
        for c, (b, h) in enumerate(chains):
            s = c // GDN_H
            u.append(_dot(tinv[c] * rw(s, SC_BETA, h), v_b[c]))
            w.append(_dot(tinv[c] * rw(s, SC_TWC, h), k_b[c]))
        state = [st_ref[b * GDN_H + h] for b, h in chains]
        ws_qs = [_dot(jnp.concatenate([w[c].astype(BF16), q_b[c]], axis=0), state[c]) for c in range(n_ch)]
        v_new = [u[c] - ws_qs[c][:TILE_T] for c in range(n_ch)]
        for c, (b, h) in enumerate(chains):
            s = c // GDN_H
            o_s[h, pl.ds(b, TILE_T, stride=SUBLANES), :] = (
                ws_qs[c][TILE_T:] * ck(s, SC_QEG, h) + _dot(a_qk[c], v_new[c]))
        for c, (b, h) in enumerate(chains):
            s = c // GDN_H
            e_last = jnp.exp(rowv[s][SC_GC][h:h + 1, TILE_T - 1:TILE_T])
            st_ref[b * GDN_H + h] = state[c] * e_last + _dot_tn(k_b[c], v_new[c] * ck(s, SC_KDL, h))
        return carry

    lax.fori_loop(0, SUBLANES // GDN_PAR, per_group, 0)

    for h in range(GDN_H):
        sl = slice(h * GDN_D, (h + 1) * GDN_D)
        y_ref[:, sl] = (_rms(o_s[h], gn_ref[...]) * _silu_half(z[:, sl])).astype(y_ref.dtype)


def _full(shape):
    nd = len(shape)
    return pl.BlockSpec(shape, lambda i: (0,) * nd, pipeline_mode=pl.Buffered(1))


def _rows(width):
    return pl.BlockSpec((ROWS, width), lambda i: (i, 0))


def _seq_params():
    return pltpu.CompilerParams(dimension_semantics=("arbitrary",), vmem_limit_bytes=VMEM_LIMIT)


def _block_diag(blocks):
    n, r, c = blocks.shape
    eye = jnp.eye(n, dtype=blocks.dtype)
    return jnp.einsum('grc,gh->grhc', blocks, eye).reshape(n * r, n * c)


def _s5_operators(a_re, a_im, b_re, b_im, c_re, c_im, log_dt):
    lam = lax.complex(a_re, a_im)
    dt = jnp.exp(log_dt)[:, None]
    lam_bar = jnp.exp(lam * dt)
    b_bar = ((lam_bar - 1.0) / lam)[..., None] * lax.complex(b_re, b_im)
    nblk = S5_W // LANES
    bt = jnp.swapaxes(b_bar, 1, 2).reshape(nblk, S5_LANE_GROUPS, S5_GROUP, S5_STATE)
    bm = jnp.concatenate([jax.vmap(_block_diag)(jnp.real(bt)), jax.vmap(_block_diag)(jnp.imag(bt))], axis=-1)
    ct = jnp.swapaxes(lax.complex(c_re, c_im), 1, 2).reshape(nblk, S5_LANE_GROUPS, S5_STATE, S5_GROUP)
    cm = jnp.concatenate([jax.vmap(_block_diag)(jnp.real(ct)), -jax.vmap(_block_diag)(jnp.imag(ct))], axis=1)
    return (bm.astype(BF16), jnp.real(lam_bar).reshape(1, -1), jnp.imag(lam_bar).reshape(1, -1),
            cm.astype(BF16))


def _seq_block(width):
    return pl.BlockSpec((SUBLANES, TILE_T, width), lambda i: (0, i, 0))


def _layer(h, p_i, prm, first, final, final_g):
    d_model = h.shape[-1]
    n_rows = h.shape[0] * h.shape[1] if first else h.shape[0]
    grid = (n_rows // ROWS,)
    (norm_g, w_in, rg_conv_w, rg_conv_b, rg_w_a, rg_b_a, rg_w_x, rg_b_x, rg_lambda,
     gdn_conv_w, gdn_a_log, gdn_dt_bias, gdn_norm_g, s5_a_re, s5_a_im, s5_b_re, s5_b_im,
     s5_c_re, s5_c_im, s5_d, s5_log_dt, s5_w_glu, s5_b_glu, w_out, ple_norm_g, ple_w_gate,
     ple_w_proj) = prm

    o_rgx, o_rgg = 0, RG_W
    o_q = 2 * RG_W
    o_z = o_q + 3 * GDN_W
    o_b = o_z + GDN_W
    o_a = o_b + GDN_H
    o_u = o_a + GDN_H
    o_sg = o_u + S5_W
    row1 = lambda v: v.reshape(1, -1).astype(F32)

    wqkv = w_in[:, o_q:o_q + 3 * GDN_W].astype(BF16)
    wz = (0.5 * w_in[:, o_z:o_z + GDN_W]).astype(BF16)
    zpad = jnp.zeros((d_model, LANES - GDN_H), w_in.dtype)
    wab = jnp.concatenate([w_in[:, o_a:o_a + GDN_H], zpad, w_in[:, o_b:o_b + GDN_H], zpad], axis=1).astype(BF16)
    lane_pad = lambda v: jnp.pad(v.reshape(1, -1).astype(F32), ((0, 0), (0, LANES - GDN_H)))
    sel = (jnp.arange(GDN_W)[:, None] // GDN_D == jnp.arange(LANES)[None, :]).astype(BF16)
    gdn_scratch = [pltpu.VMEM((HALO + ROWS, 3 * GDN_W), F32), pltpu.VMEM((ROWS, d_model), BF16),
                   pltpu.VMEM((3 * GDN_H, ROWS, GDN_D), F32),
                   pltpu.VMEM((ROWS, 2 * GDN_W), BF16),
                   pltpu.VMEM((N_SC, ROWS, LANES), F32),
                   pltpu.VMEM((GDN_H, ROWS, GDN_D), F32),
                   pltpu.VMEM((SUBLANES * GDN_H, GDN_D, GDN_D), F32)]
    y_spec, y_shape = _rows(GDN_W), jax.ShapeDtypeStruct((n_rows, GDN_W), BF16)
    if first:
        gdn_scratch.append(pltpu.VMEM((d_model // LANES, ROWS, LANES), F32))
        y_spec, y_shape = [y_spec, _rows(d_model)], [y_shape, jax.ShapeDtypeStruct((n_rows, d_model), F32)]
    y_g = pl.pallas_call(
        functools.partial(_gdn_kernel, first=first),
        grid=grid,
        in_specs=[_seq_block(d_model) if first else _rows(d_model), _full((1, d_model)), _full(wqkv.shape),
                  _full(wz.shape), _full(wab.shape), _full((CONV_K, 3 * GDN_W)), _full((1, LANES)),
                  _full((1, LANES)), _full((1, GDN_D)), _full(sel.shape)],
        out_specs=y_spec,
        out_shape=y_shape,
        scratch_shapes=gdn_scratch,
        compiler_params=_seq_params(),
        name="gdn",
    )(h, row1(norm_g), wqkv, wz, wab, 0.5 * gdn_conv_w.astype(F32), lane_pad(-jnp.exp(gdn_a_log)),
      lane_pad(gdn_dt_bias), row1(gdn_norm_g), sel)
    if first:
        y_g, h = y_g

    w_a = jnp.concatenate([w_in[:, o_rgx:o_rgx + RG_W], 0.5 * w_in[:, o_rgg:o_rgg + RG_W],
                           w_in[:, o_u:o_u + S5_W], 0.5 * w_in[:, o_sg:o_sg + S5_W]], axis=1).astype(BF16)
    nb = RG_W // LANES
    per = rg_w_a.shape[0] // nb
    wg = jnp.stack([jnp.concatenate([_block_diag(rg_w_a[j * per:(j + 1) * per]),
                                     _block_diag(rg_w_x[j * per:(j + 1) * per])], axis=1)
                    for j in range(nb)]).astype(BF16)
    c_rg = row1(-RG_C * jax.nn.softplus(-rg_lambda))
    bm, lam_re, lam_im, cm = _s5_operators(s5_a_re, s5_a_im, s5_b_re, s5_b_im, s5_c_re, s5_c_im, s5_log_dt)
    n_state = 2 * S5_BLK_STATE * (S5_W // LANES)
    wo_rg = w_out[0:RG_W].astype(BF16)
    wo_g = w_out[RG_W:RG_W + GDN_W].astype(BF16)
    wo_s5 = w_out[RG_W + GDN_W:].astype(BF16)
    d_ple = p_i.shape[-1]
    scratch = [pltpu.VMEM((HALO + ROWS, RG_W), F32), pltpu.VMEM((ROWS, RG_W), F32),
               pltpu.VMEM((ROWS, RG_W), F32), pltpu.VMEM((SUBLANES, RG_W), F32),
               pltpu.VMEM((ROWS, n_state), F32), pltpu.VMEM((SUBLANES, n_state), F32),
               pltpu.VMEM((d_ple // LANES, ROWS, LANES), F32)]
    if final:
        batch = p_i.shape[0]
        scratch.append(pltpu.VMEM((d_model // LANES, ROWS, LANES), F32))
        o_spec, o_shape = _seq_block(d_model), jax.ShapeDtypeStruct((batch, n_rows // batch, d_model), F32)
    else:
        o_spec, o_shape = _rows(d_model), jax.ShapeDtypeStruct((n_rows, d_model), F32)
    return pl.pallas_call(
        functools.partial(_rg_s5_out_kernel, final=final),
        grid=grid,
        in_specs=[_rows(d_model), _full((1, d_model)), _full(w_a.shape), _full((CONV_K, RG_W)),
                  _full((1, RG_W)), _full(wg.shape), _full((1, RG_W)), _full((1, RG_W)), _full((1, RG_W)),
                  _full(bm.shape), _full(lam_re.shape), _full(lam_im.shape), _full(cm.shape),
                  _full((1, S5_W)), _full((S5_W, S5_W)), _full((1, S5_W)),
                  _rows(GDN_W), _seq_block(d_ple), _full(wo_rg.shape), _full(wo_s5.shape), _full(wo_g.shape),
                  _full((1, d_model)), _full((d_model, d_model)), _full((d_ple, d_model)), _full((1, d_model))],
        out_specs=o_spec,
        out_shape=o_shape,
        scratch_shapes=scratch,
        compiler_params=_seq_params(),
        name="rg_s5_out",
    )(h, row1(norm_g), w_a, rg_conv_w.astype(F32), row1(rg_conv_b), wg, row1(rg_b_a), row1(rg_b_x), c_rg,
      bm, lam_re, lam_im, cm, row1(s5_d), s5_w_glu.astype(BF16), row1(s5_b_glu),
      y_g, p_i, wo_rg, wo_s5, wo_g, row1(ple_norm_g), ple_w_gate.astype(BF16), ple_w_proj.astype(BF16),
      row1(final_g))


def kernel(x, p, norm_g, w_in, rg_conv_w, rg_conv_b, rg_w_a, rg_b_a, rg_w_x, rg_b_x, rg_lambda, gdn_conv_w, gdn_a_log, gdn_dt_bias, gdn_norm_g, s5_a_re, s5_a_im, s5_b_re, s5_b_im, s5_c_re, s5_c_im, s5_d, s5_log_dt, s5_w_glu, s5_b_glu, w_out, ple_norm_g, ple_w_gate, ple_w_proj, final_norm_g):
    batch, seq, d_model = x.shape
    depth = p.shape[0]
    assert batch == SUBLANES and seq % TILE_T == 0 and depth >= 1
    assert x.dtype == F32
    layer_params = (norm_g, w_in, rg_conv_w, rg_conv_b, rg_w_a, rg_b_a, rg_w_x, rg_b_x, rg_lambda,
                    gdn_conv_w, gdn_a_log, gdn_dt_bias, gdn_norm_g, s5_a_re, s5_a_im, s5_b_re, s5_b_im,
                    s5_c_re, s5_c_im, s5_d, s5_log_dt, s5_w_glu, s5_b_glu, w_out, ple_norm_g, ple_w_gate,
                    ple_w_proj)
    h = x
    for i in range(depth):
        h = _layer(h, p[i], tuple(a[i] for a in layer_params), i == 0, i == depth - 1, final_norm_g)
    return h
```

```python
import functools

import jax
import jax.numpy as jnp
from jax import lax
from jax.experimental import pallas as pl
from jax.experimental.pallas import tpu as pltpu

F32 = jnp.float32
BF16 = jnp.bfloat16

NORM_EPS = 1e-6
RG_C = 8.0
CONV_K = 4

SUBLANES = 8
LANES = 128
TILE_T = 64
ROWS = TILE_T * SUBLANES
HALO = (CONV_K - 1) * SUBLANES

RG_W = 512
S5_W = 512
S5_GROUP = 16
S5_STATE = 64
S5_LANE_GROUPS = LANES // S5_GROUP
S5_BLK_STATE = S5_LANE_GROUPS * S5_STATE
GDN_H = 8
GDN_D = 128
GDN_W = GDN_H * GDN_D

VMEM_LIMIT = 56 * 1024 * 1024


def _rms(x, g):
    ms = jnp.mean(x * x, axis=-1, keepdims=True)
    return x * lax.rsqrt(ms + NORM_EPS) * g


def _sigmoid(x):
    return 1.0 / (1.0 + jnp.exp(-x))


def _silu(x):
    return x * _sigmoid(x)


def _silu_half(hx):
    return hx + hx * jnp.tanh(hx)


def _softplus(x):
    return jnp.maximum(x, 0.0) + jnp.log1p(jnp.exp(-jnp.abs(x)))


def _expm1(x):
    u = jnp.exp(x)
    um1 = u - 1.0
    regular = jnp.logical_and(um1 != 0.0, um1 != -1.0)
    v = um1 * x / jnp.log(jnp.where(regular, u, 2.0))
    return jnp.where(um1 == 0.0, x, jnp.where(um1 == -1.0, -1.0, v))


def _dot(a, b):
    return jnp.dot(a.astype(BF16), b.astype(BF16), preferred_element_type=F32)


def _dot_nt(a, b):
    return lax.dot_general(a.astype(BF16), b.astype(BF16), (((1,), (1,)), ((), ())),
                           preferred_element_type=F32)


def _dot_tn(a, b):
    return lax.dot_general(a.astype(BF16), b.astype(BF16), (((0,), (0,)), ((), ())),
                           preferred_element_type=F32)


def _causal_conv(x, buf, cw):
    buf[HALO:HALO + ROWS, :] = x
    y = cw[CONV_K - 1:CONV_K, :] * x
    for k in range(CONV_K - 1):
        off = HALO - (CONV_K - 1 - k) * SUBLANES
        y = y + cw[k:k + 1, :] * buf[off:off + ROWS, :]
    buf[0:HALO, :] = buf[ROWS:ROWS + HALO, :]
    return y


def _to_time_major(src_ref, chunk_buf):
    n_chunks = src_ref.shape[-1] // LANES
    for b in range(SUBLANES):
        for c in range(n_chunks):
            chunk_buf[c, pl.ds(b, TILE_T, stride=SUBLANES), :] = src_ref[b, :, c * LANES:(c + 1) * LANES]
    return jnp.concatenate([chunk_buf[c] for c in range(n_chunks)], axis=1)


def _from_time_major(val, chunk_buf, dst_ref):
    n_chunks = val.shape[-1] // LANES
    for c in range(n_chunks):
        chunk_buf[c] = val[:, c * LANES:(c + 1) * LANES]
    for b in range(SUBLANES):
        for c in range(n_chunks):
            dst_ref[b, :, c * LANES:(c + 1) * LANES] = chunk_buf[c, pl.ds(b, TILE_T, stride=SUBLANES), :]


def _rg_s5_out_kernel(h_ref, ng_ref, w_ref, cw_ref, cb_ref, wg_ref, ba_ref, bx_ref, c_ref,
                      bm_ref, lr_ref, li_ref, cm_ref, d_ref, wglu_ref, bglu_ref,
                      yg_ref, p_ref, wo_rg_ref, wo_s5_ref, wo_g_ref, png_ref, wgate_ref, wproj_ref, fg_ref,
                      o_ref,
                      xbuf, a_buf, i_buf, hcar, st_buf, scar, p_buf, *rest, final):
    @pl.when(pl.program_id(0) == 0)
    def _():
        xbuf[0:HALO, :] = jnp.zeros((HALO, RG_W), F32)
        hcar[...] = jnp.zeros_like(hcar)
        scar[...] = jnp.zeros_like(scar)

    h = h_ref[...]
    hn = _rms(h, ng_ref[...]).astype(BF16)
    w_cols = lambda c0, width: w_ref[:, c0:c0 + width]
    nblk = S5_W // LANES
    sw = 2 * S5_BLK_STATE

    def s5_scan(j):
        c_re = j * sw
        c_im = j * sw + S5_BLK_STATE
        lr = jnp.broadcast_to(lr_ref[:, j * S5_BLK_STATE:(j + 1) * S5_BLK_STATE], (SUBLANES, S5_BLK_STATE))
        li = jnp.broadcast_to(li_ref[:, j * S5_BLK_STATE:(j + 1) * S5_BLK_STATE], (SUBLANES, S5_BLK_STATE))

        def s5_step(t, carry):
            sr, si = carry
            r0 = pl.multiple_of(t * SUBLANES, SUBLANES)
            nr = lr * sr - li * si + st_buf[pl.ds(r0, SUBLANES), c_re:c_re + S5_BLK_STATE]
            ni = lr * si + li * sr + st_buf[pl.ds(r0, SUBLANES), c_im:c_im + S5_BLK_STATE]
            st_buf[pl.ds(r0, SUBLANES), c_re:c_re + S5_BLK_STATE] = nr
            st_buf[pl.ds(r0, SUBLANES), c_im:c_im + S5_BLK_STATE] = ni
            return nr, ni

        sr, si = lax.fori_loop(
            0, TILE_T, s5_step,
            (scar[:, c_re:c_re + S5_BLK_STATE], scar[:, c_im:c_im + S5_BLK_STATE]), unroll=True)
        scar[:, c_re:c_re + S5_BLK_STATE] = sr
        scar[:, c_im:c_im + S5_BLK_STATE] = si

    def s5_readout(j):
        return _dot(st_buf[:, j * sw:(j + 1) * sw], cm_ref[j])

    rg_x = jnp.dot(hn, w_cols(0, RG_W), preferred_element_type=F32)
    s5_u = jnp.dot(hn, w_cols(2 * RG_W, S5_W), preferred_element_type=F32)
    for j in range(nblk):
        st_buf[:, j * sw:(j + 1) * sw] = _dot(s5_u[:, j * LANES:(j + 1) * LANES], bm_ref[j])

    xr = _causal_conv(rg_x, xbuf, cw_ref[...]) + cb_ref[...]
    pre_a, pre_x = [], []
    for j in range(RG_W // LANES):
        g = _dot(xr[:, j * LANES:(j + 1) * LANES], wg_ref[j])
        pre_a.append(g[:, :LANES])
        pre_x.append(g[:, LANES:])

    rg_gate = jnp.dot(hn, w_cols(RG_W, RG_W), preferred_element_type=F32)
    s5_scan(0)
    s5_gate = jnp.dot(hn, w_cols(2 * RG_W + S5_W, S5_W), preferred_element_type=F32)
    s5_scan(1)
    acc = h + jnp.dot(yg_ref[...], wo_g_ref[...], preferred_element_type=F32)

    r = _sigmoid(jnp.concatenate(pre_a, axis=1) + ba_ref[...])
    gi = _sigmoid(jnp.concatenate(pre_x, axis=1) + bx_ref[...])
    log_a = c_ref[...] * r
    a_buf[...] = jnp.exp(log_a)
    i_buf[...] = jnp.sqrt(-_expm1(2.0 * log_a)) * (gi * xr)
    ys = [s5_readout(0), s5_readout(1)]
    s5_scan(2)
    s5_scan(3)

    def rg_step(t, hc):
        r0 = pl.multiple_of(t * SUBLANES, SUBLANES)
        hc = a_buf[pl.ds(r0, SUBLANES), :] * hc + i_buf[pl.ds(r0, SUBLANES), :]
        i_buf[pl.ds(r0, SUBLANES), :] = hc
        return hc

    hcar[...] = lax.fori_loop(0, TILE_T, rg_step, hcar[...], unroll=True)
    y_rg = (i_buf[...] * _silu_half(rg_gate)).astype(BF16)
    acc = acc + jnp.dot(y_rg, wo_rg_ref[...], preferred_element_type=F32)

    ys += [s5_readout(2), s5_readout(3)]
    y5 = jnp.concatenate(ys, axis=1) + d_ref[...] * s5_u
    z5 = jax.nn.gelu(y5)
    y5 = z5 * _sigmoid(_dot(z5, wglu_ref[...]) + bglu_ref[...])
    y_s5 = (y5 * _silu_half(s5_gate)).astype(BF16)

    h = acc + jnp.dot(y_s5, wo_s5_ref[...], preferred_element_type=F32)
    gate = _sigmoid(_dot(_rms(h, png_ref[...]), wgate_ref[...]))
    h = h + gate * _dot(_to_time_major(p_ref, p_buf), wproj_ref[...])
    if final:
        _from_time_major(_rms(h, fg_ref[...]), rest[0], o_ref)
    else:
        o_ref[...] = h


N_SC = 8
SC_GC, SC_BRK, SC_RQ, SC_QEG, SC_KDL, SC_RK, SC_BETA, SC_TWC = range(N_SC)
SC_COLS = (SC_GC, SC_BRK, SC_RQ, SC_QEG, SC_KDL)
SC_ROWS = (SC_GC, SC_RK, SC_BETA, SC_TWC)
CONV_RB = 32
GDN_PAR = 4


def _gdn_kernel(h_ref, ng_ref, wqkv_ref, wz_ref, wab_ref, cw_ref, nea_ref, dtb_ref, gn_ref, sel_ref,
                y_ref, *rest, first):
    if first:
        htm_ref, cbuf, hn_buf, qkv_s, sq_s, sc_s, o_s, st_ref, chunk_buf = rest
    else:
        cbuf, hn_buf, qkv_s, sq_s, sc_s, o_s, st_ref = rest

    @pl.when(pl.program_id(0) == 0)
    def _():
        cbuf[0:HALO, :] = jnp.zeros((HALO, 3 * GDN_W), F32)
        st_ref[...] = jnp.zeros_like(st_ref)

    if first:
        h = _to_time_major(h_ref, chunk_buf)
        htm_ref[...] = h
    else:
        h = h_ref[...]
    hn_buf[...] = _rms(h, ng_ref[...]).astype(BF16)

    ab = jnp.dot(hn_buf[...], wab_ref[...], preferred_element_type=F32)
    g = nea_ref[...] * _softplus(ab[:, :LANES] + dtb_ref[...])
    beta = _sigmoid(ab[:, LANES:])
    gc = g
    d = 1
    while d < TILE_T:
        sh = d * SUBLANES
        gc = gc + jnp.concatenate([jnp.zeros((sh, LANES), F32), gc[:ROWS - sh, :]], axis=0)
        d *= 2
    g_last = jnp.broadcast_to(gc[ROWS - SUBLANES:, :][None], (TILE_T, SUBLANES, LANES)).reshape(ROWS, LANES)

    def project(c3):
        cbuf[HALO:HALO + ROWS, c3 * GDN_W:(c3 + 1) * GDN_W] = jnp.dot(
            hn_buf[...], wqkv_ref[:, c3 * GDN_W:(c3 + 1) * GDN_W], preferred_element_type=F32)

    def conv_group(c3):
        for c in range(c3 * GDN_H, (c3 + 1) * GDN_H):
            cols = slice(c * GDN_D, (c + 1) * GDN_D)
            taps = [cw_ref[k:k + 1, cols] for k in range(CONV_K)]
            for r0 in range(0, ROWS, CONV_RB):
                y = taps[CONV_K - 1] * cbuf[r0 + HALO:r0 + HALO + CONV_RB, cols]
                for k in range(CONV_K - 1):
                    off = r0 + HALO - (CONV_K - 1 - k) * SUBLANES
                    y = y + taps[k] * cbuf[off:off + CONV_RB, cols]
                act = _silu_half(y)
                qkv_s[c, r0:r0 + CONV_RB, :] = act
                if c3 < 2:
                    sq_s[r0:r0 + CONV_RB, cols] = (act * act).astype(BF16)

    project(0)
    project(1)
    conv_group(0)
    project(2)
    conv_group(1)
    z = jnp.dot(hn_buf[...], wz_ref[...], preferred_element_type=F32)
    conv_group(2)
    halo = cbuf[ROWS:ROWS + HALO, :]
    cbuf[0:HALO, :] = halo

    ss_q = jnp.dot(sq_s[:, 0:GDN_W], sel_ref[...], preferred_element_type=F32)
    ss_k = jnp.dot(sq_s[:, GDN_W:2 * GDN_W], sel_ref[...], preferred_element_type=F32)
    rq = lax.rsqrt(ss_q + NORM_EPS) * (GDN_D ** -0.5)
    rk = lax.rsqrt(ss_k + NORM_EPS)
    egc = jnp.exp(gc)
    sc_s[SC_GC] = gc
    sc_s[SC_BRK] = beta * rk
    sc_s[SC_RQ] = rq
    sc_s[SC_QEG] = rq * egc
    sc_s[SC_KDL] = rk * jnp.exp(g_last - gc)
    sc_s[SC_RK] = rk
    sc_s[SC_BETA] = beta
    sc_s[SC_TWC] = beta * rk * egc

    row = lax.broadcasted_iota(jnp.int32, (TILE_T, TILE_T), 0)
    col = lax.broadcasted_iota(jnp.int32, (TILE_T, TILE_T), 1)
    causal = row >= col
    strict = row > col
    eye = (row == col).astype(F32)

    def per_group(i, carry):
        chains = [(i * GDN_PAR + s, h) for s in range(GDN_PAR) for h in range(GDN_H)]
        colv, rowv = [], []
        for s in range(GDN_PAR):
            rows = pl.ds(i * GDN_PAR + s, TILE_T, stride=SUBLANES)
            colv.append({n: sc_s[n, rows, :] for n in SC_COLS})
            pad = jnp.zeros((TILE_T, LANES), F32)
            rowv.append({n: jnp.concatenate([sc_s[n, rows, :], pad], axis=0).T for n in SC_ROWS})
        ck = lambda s, n, h: colv[s][n][:, h:h + 1]
        rw = lambda s, n, h: rowv[s][n][h:h + 1, 0:TILE_T]

        q_b, k_b, v_b = [], [], []
        for b, h in chains:
            rows = pl.ds(b, TILE_T, stride=SUBLANES)
            q_b.append(qkv_s[h, rows, :].astype(BF16))
            k_b.append(qkv_s[GDN_H + h, rows, :].astype(BF16))
            v_b.append(qkv_s[2 * GDN_H + h, rows, :].astype(BF16))
        n_ch = len(chains)
        kk = [_dot_nt(jnp.concatenate([k_b[c], q_b[c]], axis=0), k_b[c]) for c in range(n_ch)]
        a_qk, neg = [], []
        for c, (b, h) in enumerate(chains):
            s = c // GDN_H
            diff = ck(s, SC_GC, h) - rw(s, SC_GC, h)
            dk = jnp.where(causal, jnp.exp(jnp.where(causal, diff, 0.0)), 0.0) * rw(s, SC_RK, h)
            neg.append(jnp.where(strict, -(kk[c][:TILE_T] * ck(s, SC_BRK, h)) * dk, 0.0))
            a_qk.append(kk[c][TILE_T:] * ck(s, SC_RQ, h) * dk)
        tinv = [eye + n for n in neg]
        pw = [_dot(n, n) for n in neg]
        span = 2
        while span < TILE_T:
            tinv = [t + _dot(t, p) for t, p in zip(tinv, pw)]
            span *= 2
            if span < TILE_T:
                pw = [_dot(p, p) for p in pw]
        u, w = [], []
        for c, (b, h) in enumerate(chains):
            s = c // GDN_H
            u.append(_dot(tinv[c] * rw(s, SC_BETA, h), v_b[c]))
            w.append(_dot(tinv[c] * rw(s, SC_TWC, h), k_b[c]))
        state = [st_ref[b * GDN_H + h] for b, h in chains]
        ws_qs = [_dot(jnp.concatenate([w[c].astype(BF16), q_b[c]], axis=0), state[c]) for c in range(n_ch)]
        v_new = [u[c] - ws_qs[c][:TILE_T] for c in range(n_ch)]
        for c, (b, h) in enumerate(chains):
            s = c // GDN_H
            o_s[h, pl.ds(b, TILE_T, stride=SUBLANES), :] = (
                ws_qs[c][TILE_T:] * ck(s, SC_QEG, h) + _dot(a_qk[c], v_new[c]))
        for c, (b, h) in enumerate(chains):
            s = c // GDN_H
            e_last = jnp.exp(rowv[s][SC_GC][h:h + 1, TILE_T - 1:TILE_T])
            st_ref[b * GDN_H + h] = state[c] * e_last + _dot_tn(k_b[c], v_new[c] * ck(s, SC_KDL, h))
        return carry

    lax.fori_loop(0, SUBLANES // GDN_PAR, per_group, 0)

    for h in range(GDN_H):
        sl = slice(h * GDN_D, (h + 1) * GDN_D)
        y_ref[:, sl] = (_rms(o_s[h], gn_ref[...]) * _silu_half(z[:, sl])).astype(y_ref.dtype)


def _full(shape):
    nd = len(shape)
    return pl.BlockSpec(shape, lambda i: (0,) * nd, pipeline_mode=pl.Buffered(1))


def _rows(width):
    return pl.BlockSpec((ROWS, width), lambda i: (i, 0))


def _seq_params():
    return pltpu.CompilerParams(dimension_semantics=("arbitrary",), vmem_limit_bytes=VMEM_LIMIT)


def _block_diag(blocks):
    n, r, c = blocks.shape
    eye = jnp.eye(n, dtype=blocks.dtype)
    return jnp.einsum('grc,gh->grhc', blocks, eye).reshape(n * r, n * c)


def _s5_operators(a_re, a_im, b_re, b_im, c_re, c_im, log_dt):
    lam = lax.complex(a_re, a_im)
    dt = jnp.exp(log_dt)[:, None]
    lam_bar = jnp.exp(lam * dt)
    b_bar = ((lam_bar - 1.0) / lam)[..., None] * lax.complex(b_re, b_im)
    nblk = S5_W // LANES
    bt = jnp.swapaxes(b_bar, 1, 2).reshape(nblk, S5_LANE_GROUPS, S5_GROUP, S5_STATE)
    bm = jnp.concatenate([jax.vmap(_block_diag)(jnp.real(bt)), jax.vmap(_block_diag)(jnp.imag(bt))], axis=-1)
    ct = jnp.swapaxes(lax.complex(c_re, c_im), 1, 2).reshape(nblk, S5_LANE_GROUPS, S5_STATE, S5_GROUP)
    cm = jnp.concatenate([jax.vmap(_block_diag)(jnp.real(ct)), -jax.vmap(_block_diag)(jnp.imag(ct))], axis=1)
    return (bm.astype(BF16), jnp.real(lam_bar).reshape(1, -1), jnp.imag(lam_bar).reshape(1, -1),
            cm.astype(BF16))


def _seq_block(width):
    return pl.BlockSpec((SUBLANES, TILE_T, width), lambda i: (0, i, 0))


def _layer(h, p_i, prm, first, final, final_g):
    d_model = h.shape[-1]
    n_rows = h.shape[0] * h.shape[1] if first else h.shape[0]
    grid = (n_rows // ROWS,)
    (norm_g, w_in, rg_conv_w, rg_conv_b, rg_w_a, rg_b_a, rg_w_x, rg_b_x, rg_lambda,
     gdn_conv_w, gdn_a_log, gdn_dt_bias, gdn_norm_g, s5_a_re, s5_a_im, s5_b_re, s5_b_im,
     s5_c_re, s5_c_im, s5_d, s5_log_dt, s5_w_glu, s5_b_glu, w_out, ple_norm_g, ple_w_gate,
     ple_w_proj) = prm

    o_rgx, o_rgg = 0, RG_W
    o_q = 2 * RG_W
    o_z = o_q + 3 * GDN_W
    o_b = o_z + GDN_W
    o_a = o_b + GDN_H
    o_u = o_a + GDN_H
    o_sg = o_u + S5_W
    row1 = lambda v: v.reshape(1, -1).astype(F32)

    wqkv = w_in[:, o_q:o_q + 3 * GDN_W].astype(BF16)
    wz = (0.5 * w_in[:, o_z:o_z + GDN_W]).astype(BF16)
    zpad = jnp.zeros((d_model, LANES - GDN_H), w_in.dtype)
    wab = jnp.concatenate([w_in[:, o_a:o_a + GDN_H], zpad, w_in[:, o_b:o_b + GDN_H], zpad], axis=1).astype(BF16)
    lane_pad = lambda v: jnp.pad(v.reshape(1, -1).astype(F32), ((0, 0), (0, LANES - GDN_H)))
    sel = (jnp.arange(GDN_W)[:, None] // GDN_D == jnp.arange(LANES)[None, :]).astype(BF16)
    gdn_scratch = [pltpu.VMEM((HALO + ROWS, 3 * GDN_W), F32), pltpu.VMEM((ROWS, d_model), BF16),
                   pltpu.VMEM((3 * GDN_H, ROWS, GDN_D), F32),
                   pltpu.VMEM((ROWS, 2 * GDN_W), BF16),
                   pltpu.VMEM((N_SC, ROWS, LANES), F32),
                   pltpu.VMEM((GDN_H, ROWS, GDN_D), F32),
                   pltpu.VMEM((SUBLANES * GDN_H, GDN_D, GDN_D), F32)]
    y_spec, y_shape = _rows(GDN_W), jax.ShapeDtypeStruct((n_rows, GDN_W), BF16)
    if first:
        gdn_scratch.append(pltpu.VMEM((d_model // LANES, ROWS, LANES), F32))
        y_spec, y_shape = [y_spec, _rows(d_model)], [y_shape, jax.ShapeDtypeStruct((n_rows, d_model), F32)]
    y_g = pl.pallas_call(
        functools.partial(_gdn_kernel, first=first),
        grid=grid,
        in_specs=[_seq_block(d_model) if first else _rows(d_model), _full((1, d_model)), _full(wqkv.shape),
                  _full(wz.shape), _full(wab.shape), _full((CONV_K, 3 * GDN_W)), _full((1, LANES)),
                  _full((1, LANES)), _full((1, GDN_D)), _full(sel.shape)],
        out_specs=y_spec,
        out_shape=y_shape,
        scratch_shapes=gdn_scratch,
        compiler_params=_seq_params(),
        name="gdn",
    )(h, row1(norm_g), wqkv, wz, wab, 0.5 * gdn_conv_w.astype(F32), lane_pad(-jnp.exp(gdn_a_log)),
      lane_pad(gdn_dt_bias), row1(gdn_norm_g), sel)
    if first:
        y_g, h = y_g

    w_a = jnp.concatenate([w_in[:, o_rgx:o_rgx + RG_W], 0.5 * w_in[:, o_rgg:o_rgg + RG_W],
                           w_in[:, o_u:o_u + S5_W], 0.5 * w_in[:, o_sg:o_sg + S5_W]], axis=1).astype(BF16)
    nb = RG_W // LANES
    per = rg_w_a.shape[0] // nb
    wg = jnp.stack([jnp.concatenate([_block_diag(rg_w_a[j * per:(j + 1) * per]),
                                     _block_diag(rg_w_x[j * per:(j + 1) * per])], axis=1)
                    for j in range(nb)]).astype(BF16)
    c_rg = row1(-RG_C * jax.nn.softplus(-rg_lambda))
    bm, lam_re, lam_im, cm = _s5_operators(s5_a_re, s5_a_im, s5_b_re, s5_b_im, s5_c_re, s5_c_im, s5_log_dt)
    n_state = 2 * S5_BLK_STATE * (S5_W // LANES)
    wo_rg = w_out[0:RG_W].astype(BF16)
    wo_g = w_out[RG_W:RG_W + GDN_W].astype(BF16)
    wo_s5 = w_out[RG_W + GDN_W:].astype(BF16)
    d_ple = p_i.shape[-1]
    scratch = [pltpu.VMEM((HALO + ROWS, RG_W), F32), pltpu.VMEM((ROWS, RG_W), F32),
               pltpu.VMEM((ROWS, RG_W), F32), pltpu.VMEM((SUBLANES, RG_W), F32),
               pltpu.VMEM((ROWS, n_state), F32), pltpu.VMEM((SUBLANES, n_state), F32),
               pltpu.VMEM((d_ple // LANES, ROWS, LANES), F32)]
    if final:
        batch = p_i.shape[0]
        scratch.append(pltpu.VMEM((d_model // LANES, ROWS, LANES), F32))
        o_spec, o_shape = _seq_block(d_model), jax.ShapeDtypeStruct((batch, n_rows // batch, d_model), F32)
    else:
        o_spec, o_shape = _rows(d_model), jax.ShapeDtypeStruct((n_rows, d_model), F32)
    return pl.pallas_call(
        functools.partial(_rg_s5_out_kernel, final=final),
        grid=grid,
        in_specs=[_rows(d_model), _full((1, d_model)), _full(w_a.shape), _full((CONV_K, RG_W)),
                  _full((1, RG_W)), _full(wg.shape), _full((1, RG_W)), _full((1, RG_W)), _full((1, RG_W)),
                  _full(bm.shape), _full(lam_re.shape), _full(lam_im.shape), _full(cm.shape),
                  _full((1, S5_W)), _full((S5_W, S5_W)), _full((1, S5_W)),
                  _rows(GDN_W), _seq_block(d_ple), _full(wo_rg.shape), _full(wo_s5.shape), _full(wo_g.shape),
                  _full((1, d_model)), _full((d_model, d_model)), _full((d_ple, d_model)), _full((1, d_model))],
        out_specs=o_spec,
        out_shape=o_shape,
        scratch_shapes=scratch,
        compiler_params=_seq_params(),
        name="rg_s5_out",
    )(h, row1(norm_g), w_a, rg_conv_w.astype(F32), row1(rg_conv_b), wg, row1(rg_b_a), row1(rg_b_x), c_rg,
      bm, lam_re, lam_im, cm, row1(s5_d), s5_w_glu.astype(BF16), row1(s5_b_glu),
      y_g, p_i, wo_rg, wo_s5, wo_g, row1(ple_norm_g), ple_w_gate.astype(BF16), ple_w_proj.astype(BF16),
      row1(final_g))


def kernel(x, p, norm_g, w_in, rg_conv_w, rg_conv_b, rg_w_a, rg_b_a, rg_w_x, rg_b_x, rg_lambda, gdn_conv_w, gdn_a_log, gdn_dt_bias, gdn_norm_g, s5_a_re, s5_a_im, s5_b_re, s5_b_im, s5_c_re, s5_c_im, s5_d, s5_log_dt, s5_w_glu, s5_b_glu, w_out, ple_norm_g, ple_w_gate, ple_w_proj, final_norm_g):
    batch, seq, d_model = x.shape
    depth = p.shape[0]
    assert batch == SUBLANES and seq % TILE_T == 0 and depth >= 1
    assert x.dtype == F32
    layer_params = (norm_g, w_in, rg_conv_w, rg_conv_b, rg_w_a, rg_b_a, rg_w_x, rg_b_x, rg_lambda,
                    gdn_conv_w, gdn_a_log, gdn_dt_bias, gdn_norm_g, s5_a_re, s5_a_im, s5_b_re, s5_b_im,
                    s5_c_re, s5_c_im, s5_d, s5_log_dt, s5_w_glu, s5_b_glu, w_out, ple_norm_g, ple_w_gate,
                    ple_w_proj)
    h = x
    for i in range(depth):
        h = _layer(h, p[i], tuple(a[i] for a in layer_params), i == 0, i == depth - 1, final_norm_g)
    return h
```

```python
import functools

import jax
import jax.numpy as jnp
from jax import lax
from jax.experimental import pallas as pl
from jax.experimental.pallas import tpu as pltpu

F32 = jnp.float32
BF16 = jnp.bfloat16

NORM_EPS = 1e-6
RG_C = 8.0
CONV_K = 4

SUBLANES = 8
LANES = 128
TILE_T = 64
ROWS = TILE_T * SUBLANES
HALO = (CONV_K - 1) * SUBLANES

RG_W = 512
S5_W = 512
S5_GROUP = 16
S5_STATE = 64
S5_LANE_GROUPS = LANES // S5_GROUP
S5_BLK_STATE = S5_LANE_GROUPS * S5_STATE
GDN_H = 8
GDN_D = 128
GDN_W = GDN_H * GDN_D

VMEM_LIMIT = 56 * 1024 * 1024


def _rms(x, g):
    ms = jnp.mean(x * x, axis=-1, keepdims=True)
    return x * lax.rsqrt(ms + NORM_EPS) * g


def _sigmoid(x):
    return 1.0 / (1.0 + jnp.exp(-x))


def _silu(x):
    return x * _sigmoid(x)


def _silu_half(hx):
    return hx + hx * jnp.tanh(hx)


def _softplus(x):
    return jnp.maximum(x, 0.0) + jnp.log1p(jnp.exp(-jnp.abs(x)))


def _expm1(x):
    u = jnp.exp(x)
    um1 = u - 1.0
    regular = jnp.logical_and(um1 != 0.0, um1 != -1.0)
    v = um1 * x / jnp.log(jnp.where(regular, u, 2.0))
    return jnp.where(um1 == 0.0, x, jnp.where(um1 == -1.0, -1.0, v))


def _dot(a, b):
    return jnp.dot(a.astype(BF16), b.astype(BF16), preferred_element_type=F32)


def _dot_nt(a, b):
    return lax.dot_general(a.astype(BF16), b.astype(BF16), (((1,), (1,)), ((), ())),
                           preferred_element_type=F32)


def _dot_tn(a, b):
    return lax.dot_general(a.astype(BF16), b.astype(BF16), (((0,), (0,)), ((), ())),
                           preferred_element_type=F32)


def _causal_conv(x, buf, cw):
    buf[HALO:HALO + ROWS, :] = x
    y = cw[CONV_K - 1:CONV_K, :] * x
    for k in range(CONV_K - 1):
        off = HALO - (CONV_K - 1 - k) * SUBLANES
        y = y + cw[k:k + 1, :] * buf[off:off + ROWS, :]
    buf[0:HALO, :] = buf[ROWS:ROWS + HALO, :]
    return y


def _to_time_major(src_ref, chunk_buf):
    n_chunks = src_ref.shape[-1] // LANES
    for b in range(SUBLANES):
        for c in range(n_chunks):
            chunk_buf[c, pl.ds(b, TILE_T, stride=SUBLANES), :] = src_ref[b, :, c * LANES:(c + 1) * LANES]
    return jnp.concatenate([chunk_buf[c] for c in range(n_chunks)], axis=1)


def _from_time_major(val, chunk_buf, dst_ref):
    n_chunks = val.shape[-1] // LANES
    for c in range(n_chunks):
        chunk_buf[c] = val[:, c * LANES:(c + 1) * LANES]
    for b in range(SUBLANES):
        for c in range(n_chunks):
            dst_ref[b, :, c * LANES:(c + 1) * LANES] = chunk_buf[c, pl.ds(b, TILE_T, stride=SUBLANES), :]


def _rg_s5_out_kernel(h_ref, ng_ref, w_ref, cw_ref, cb_ref, wg_ref, ba_ref, bx_ref, c_ref,
                      bm_ref, lr_ref, li_ref, cm_ref, d_ref, wglu_ref, bglu_ref,
                      yg_ref, p_ref, wo_rg_ref, wo_s5_ref, wo_g_ref, png_ref, wgate_ref, wproj_ref, fg_ref,
                      o_ref,
                      xbuf, a_buf, i_buf, hcar, st_buf, scar, p_buf, *rest, final):
    @pl.when(pl.program_id(0) == 0)
    def _():
        xbuf[0:HALO, :] = jnp.zeros((HALO, RG_W), F32)
        hcar[...] = jnp.zeros_like(hcar)
        scar[...] = jnp.zeros_like(scar)

    h = h_ref[...]
    hn = _rms(h, ng_ref[...]).astype(BF16)
    w_cols = lambda c0, width: w_ref[:, c0:c0 + width]
    nblk = S5_W // LANES
    sw = 2 * S5_BLK_STATE

    def s5_scan(j):
        c_re = j * sw
        c_im = j * sw + S5_BLK_STATE
        lr = jnp.broadcast_to(lr_ref[:, j * S5_BLK_STATE:(j + 1) * S5_BLK_STATE], (SUBLANES, S5_BLK_STATE))
        li = jnp.broadcast_to(li_ref[:, j * S5_BLK_STATE:(j + 1) * S5_BLK_STATE], (SUBLANES, S5_BLK_STATE))

        def s5_step(t, carry):
            sr, si = carry
            r0 = pl.multiple_of(t * SUBLANES, SUBLANES)
            nr = lr * sr - li * si + st_buf[pl.ds(r0, SUBLANES), c_re:c_re + S5_BLK_STATE]
            ni = lr * si + li * sr + st_buf[pl.ds(r0, SUBLANES), c_im:c_im + S5_BLK_STATE]
            st_buf[pl.ds(r0, SUBLANES), c_re:c_re + S5_BLK_STATE] = nr
            st_buf[pl.ds(r0, SUBLANES), c_im:c_im + S5_BLK_STATE] = ni
            return nr, ni

        sr, si = lax.fori_loop(
            0, TILE_T, s5_step,
            (scar[:, c_re:c_re + S5_BLK_STATE], scar[:, c_im:c_im + S5_BLK_STATE]), unroll=True)
        scar[:, c_re:c_re + S5_BLK_STATE] = sr
        scar[:, c_im:c_im + S5_BLK_STATE] = si

    def s5_readout(j):
        return _dot(st_buf[:, j * sw:(j + 1) * sw], cm_ref[j])

    rg_x = jnp.dot(hn, w_cols(0, RG_W), preferred_element_type=F32)
    s5_u = jnp.dot(hn, w_cols(2 * RG_W, S5_W), preferred_element_type=F32)
    for j in range(nblk):
        st_buf[:, j * sw:(j + 1) * sw] = _dot(s5_u[:, j * LANES:(j + 1) * LANES], bm_ref[j])

    xr = _causal_conv(rg_x, xbuf, cw_ref[...]) + cb_ref[...]
    pre_a, pre_x = [], []
    for j in range(RG_W // LANES):
        g = _dot(xr[:, j * LANES:(j + 1) * LANES], wg_ref[j])
        pre_a.append(g[:, :LANES])
        pre_x.append(g[:, LANES:])

    rg_gate = jnp.dot(hn, w_cols(RG_W, RG_W), preferred_element_type=F32)
    s5_scan(0)
    s5_gate = jnp.dot(hn, w_cols(2 * RG_W + S5_W, S5_W), preferred_element_type=F32)
    s5_scan(1)
    acc = h + jnp.dot(yg_ref[...], wo_g_ref[...], preferred_element_type=F32)

    r = _sigmoid(jnp.concatenate(pre_a, axis=1) + ba_ref[...])
    gi = _sigmoid(jnp.concatenate(pre_x, axis=1) + bx_ref[...])
    log_a = c_ref[...] * r
    a_buf[...] = jnp.exp(log_a)
    i_buf[...] = jnp.sqrt(-_expm1(2.0 * log_a)) * (gi * xr)
    ys = [s5_readout(0), s5_readout(1)]
    s5_scan(2)
    s5_scan(3)

    def rg_step(t, hc):
        r0 = pl.multiple_of(t * SUBLANES, SUBLANES)
        hc = a_buf[pl.ds(r0, SUBLANES), :] * hc + i_buf[pl.ds(r0, SUBLANES), :]
        i_buf[pl.ds(r0, SUBLANES), :] = hc
        return hc

    hcar[...] = lax.fori_loop(0, TILE_T, rg_step, hcar[...], unroll=True)
    y_rg = (i_buf[...] * _silu_half(rg_gate)).astype(BF16)
    acc = acc + jnp.dot(y_rg, wo_rg_ref[...], preferred_element_type=F32)

    ys += [s5_readout(2), s5_readout(3)]
    y5 = jnp.concatenate(ys, axis=1) + d_ref[...] * s5_u
    z5 = jax.nn.gelu(y5)
    y5 = z5 * _sigmoid(_dot(z5, wglu_ref[...]) + bglu_ref[...])
    y_s5 = (y5 * _silu_half(s5_gate)).astype(BF16)

    h = acc + jnp.dot(y_s5, wo_s5_ref[...], preferred_element_type=F32)
    gate = _sigmoid(_dot(_rms(h, png_ref[...]), wgate_ref[...]))
    h = h + gate * _dot(_to_time_major(p_ref, p_buf), wproj_ref[...])
    if final:
        _from_time_major(_rms(h, fg_ref[...]), rest[0], o_ref)
    else:
        o_ref[...] = h


N_SC = 8
SC_GC, SC_BRK, SC_RQ, SC_QEG, SC_KDL, SC_RK, SC_BETA, SC_TWC = range(N_SC)
SC_COLS = (SC_GC, SC_BRK, SC_RQ, SC_QEG, SC_KDL)
SC_ROWS = (SC_GC, SC_RK, SC_BETA, SC_TWC)
CONV_RB = 32
GDN_PAR = 4


def _gdn_kernel(h_ref, ng_ref, wqkv_ref, wz_ref, wab_ref, cw_ref, nea_ref, dtb_ref, gn_ref, sel_ref,
                y_ref, *rest, first):
    if first:
        htm_ref, cbuf, hn_buf, qkv_s, sq_s, sc_s, o_s, st_ref, chunk_buf = rest
    else:
        cbuf, hn_buf, qkv_s, sq_s, sc_s, o_s, st_ref = rest

    @pl.when(pl.program_id(0) == 0)
    def _():
        cbuf[0:HALO, :] = jnp.zeros((HALO, 3 * GDN_W), F32)
        st_ref[...] = jnp.zeros_like(st_ref)

    if first:
        h = _to_time_major(h_ref, chunk_buf)
        htm_ref[...] = h
    else:
        h = h_ref[...]
    hn_buf[...] = _rms(h, ng_ref[...]).astype(BF16)

    ab = jnp.dot(hn_buf[...], wab_ref[...], preferred_element_type=F32)
    g = nea_ref[...] * _softplus(ab[:, :LANES] + dtb_ref[...])
    beta = _sigmoid(ab[:, LANES:])
    gc = g
    d = 1
    while d < TILE_T:
        sh = d * SUBLANES
        gc = gc + jnp.concatenate([jnp.zeros((sh, LANES), F32), gc[:ROWS - sh, :]], axis=0)
        d *= 2
    g_last = jnp.broadcast_to(gc[ROWS - SUBLANES:, :][None], (TILE_T, SUBLANES, LANES)).reshape(ROWS, LANES)

    def project(c3):
        cbuf[HALO:HALO + ROWS, c3 * GDN_W:(c3 + 1) * GDN_W] = jnp.dot(
            hn_buf[...], wqkv_ref[:, c3 * GDN_W:(c3 + 1) * GDN_W], preferred_element_type=F32)

    def conv_group(c3):
        for c in range(c3 * GDN_H, (c3 + 1) * GDN_H):
            cols = slice(c * GDN_D, (c + 1) * GDN_D)
            taps = [cw_ref[k:k + 1, cols] for k in range(CONV_K)]
            for r0 in range(0, ROWS, CONV_RB):
                y = taps[CONV_K - 1] * cbuf[r0 + HALO:r0 + HALO + CONV_RB, cols]
                for k in range(CONV_K - 1):
                    off = r0 + HALO - (CONV_K - 1 - k) * SUBLANES
                    y = y + taps[k] * cbuf[off:off + CONV_RB, cols]
                act = _silu_half(y)
                qkv_s[c, r0:r0 + CONV_RB, :] = act
                if c3 < 2:
                    sq_s[r0:r0 + CONV_RB, cols] = (act * act).astype(BF16)

    project(0)
    project(1)
    conv_group(0)
    project(2)
    conv_group(1)
    z = jnp.dot(hn_buf[...], wz_ref[...], preferred_element_type=F32)
    conv_group(2)
    halo = cbuf[ROWS:ROWS + HALO, :]
    cbuf[0:HALO, :] = halo

    ss_q = jnp.dot(sq_s[:, 0:GDN_W], sel_ref[...], preferred_element_type=F32)
    ss_k = jnp.dot(sq_s[:, GDN_W:2 * GDN_W], sel_ref[...], preferred_element_type=F32)
    rq = lax.rsqrt(ss_q + NORM_EPS) * (GDN_D ** -0.5)
    rk = lax.rsqrt(ss_k + NORM_EPS)
    egc = jnp.exp(gc)
    sc_s[SC_GC] = gc
    sc_s[SC_BRK] = beta * rk
    sc_s[SC_RQ] = rq
    sc_s[SC_QEG] = rq * egc
    sc_s[SC_KDL] = rk * jnp.exp(g_last - gc)
    sc_s[SC_RK] = rk
    sc_s[SC_BETA] = beta
    sc_s[SC_TWC] = beta * rk * egc

    row = lax.broadcasted_iota(jnp.int32, (TILE_T, TILE_T), 0)
    col = lax.broadcasted_iota(jnp.int32, (TILE_T, TILE_T), 1)
    causal = row >= col
    strict = row > col
    eye = (row == col).astype(F32)

    def per_group(i, carry):
        chains = [(i * GDN_PAR + s, h) for s in range(GDN_PAR) for h in range(GDN_H)]
        colv, rowv = [], []
        for s in range(GDN_PAR):
            rows = pl.ds(i * GDN_PAR + s, TILE_T, stride=SUBLANES)
            colv.append({n: sc_s[n, rows, :] for n in SC_COLS})
            pad = jnp.zeros((TILE_T, LANES), F32)
            rowv.append({n: jnp.concatenate([sc_s[n, rows, :], pad], axis=0).T for n in SC_ROWS})
        ck = lambda s, n, h: colv[s][n][:, h:h + 1]
        rw = lambda s, n, h: rowv[s][n][h:h + 1, 0:TILE_T]

        q_b, k_b, v_b = [], [], []
        for b, h in chains:
            rows = pl.ds(b, TILE_T, stride=SUBLANES)
            q_b.append(qkv_s[h, rows, :].astype(BF16))
            k_b.append(qkv_s[GDN_H + h, rows, :].astype(BF16))
            v_b.append(qkv_s[2 * GDN_H + h, rows, :].astype(BF16))
        n_ch = len(chains)
        kk = [_dot_nt(jnp.concatenate([k_b[c], q_b[c]], axis=0), k_b[c]) for c in range(n_ch)]
        a_qk, neg = [], []
        for c, (b, h) in enumerate(chains):
            s = c // GDN_H
            diff = ck(s, SC_GC, h) - rw(s, SC_GC, h)
            dk = jnp.where(causal, jnp.exp(jnp.where(causal, diff, 0.0)), 0.0) * rw(s, SC_RK, h)
            neg.append(jnp.where(strict, -(kk[c][:TILE_T] * ck(s, SC_BRK, h)) * dk, 0.0))
            a_qk.append(kk[c][TILE_T:] * ck(s, SC_RQ, h) * dk)
        tinv = [eye + n for n in neg]
        pw = [_dot(n, n) for n in neg]
        span = 2
        while span < TILE_T:
            tinv = [t + _dot(t, p) for t, p in zip(tinv, pw)]
            span *= 2
            if span < TILE_T:
                pw = [_dot(p, p) for p in pw]
        u, w = [], []
        for c, (b, h) in enumerate(chains):
            s = c // GDN_H
            u.append(_dot(tinv[c] * rw(s, SC_BETA, h), v_b[c]))
            w.append(_dot(tinv[c] * rw(s, SC_TWC, h), k_b[c]))
        state = [st_ref[b * GDN_H + h] for b, h in chains]
        ws_qs = [_dot(jnp.concatenate([w[c].astype(BF16), q_b[c]], axis=0), state[c]) for c in range(n_ch)]
        v_new = [u[c] - ws_qs[c][:TILE_T] for c in range(n_ch)]
        for c, (b, h) in enumerate(chains):
            s = c // GDN_H
            o_s[h, pl.ds(b, TILE_T, stride=SUBLANES), :] = (
                ws_qs[c][TILE_T:] * ck(s, SC_QEG, h) + _dot(a_qk[c], v_new[c]))
        for c, (b, h) in enumerate(chains):
            s = c // GDN_H
            e_last = jnp.exp(rowv[s][SC_GC][h:h + 1, TILE_T - 1:TILE_T])
            st_ref[b * GDN_H + h] = state[c] * e_last + _dot_tn(k_b[c], v_new[c] * ck(s, SC_KDL, h))
        return carry

    lax.fori_loop(0, SUBLANES // GDN_PAR, per_group, 0)

    for h in range(GDN_H):
        sl = slice(h * GDN_D, (h + 1) * GDN_D)
        y_ref[:, sl] = (_rms(o_s[h], gn_ref[...]) * _silu_half(z[:, sl])).astype(y_ref.dtype)


def _layer_const(arr, layer):
    nd = arr.ndim - 1
    return pl.BlockSpec((None,) + arr.shape[1:], lambda i: (layer,) + (0,) * nd, pipeline_mode=pl.Buffered(1))


def _full(shape):
    nd = len(shape)
    return pl.BlockSpec(shape, lambda i: (0,) * nd, pipeline_mode=pl.Buffered(1))


def _rows(width):
    return pl.BlockSpec((ROWS, width), lambda i: (i, 0))


def _seq_block(width):
    return pl.BlockSpec((SUBLANES, TILE_T, width), lambda i: (0, i, 0))


def _seq_params():
    return pltpu.CompilerParams(dimension_semantics=("arbitrary",), vmem_limit_bytes=VMEM_LIMIT)


def _block_diag(blocks):
    n, r, c = blocks.shape[-3:]
    eye = jnp.eye(n, dtype=blocks.dtype)
    return jnp.einsum('...grc,gh->...grhc', blocks, eye).reshape(blocks.shape[:-3] + (n * r, n * c))


def _s5_operators(a_re, a_im, b_re, b_im, c_re, c_im, log_dt):
    depth = a_re.shape[0]
    lam = lax.complex(a_re, a_im)
    dt = jnp.exp(log_dt)[..., None]
    lam_bar = jnp.exp(lam * dt)
    b_bar = ((lam_bar - 1.0) / lam)[..., None] * lax.complex(b_re, b_im)
    nblk = S5_W // LANES
    bt = jnp.swapaxes(b_bar, 2, 3).reshape(depth, nblk, S5_LANE_GROUPS, S5_GROUP, S5_STATE)
    bm = jnp.concatenate([_block_diag(jnp.real(bt)), _block_diag(jnp.imag(bt))], axis=-1)
    ct = jnp.swapaxes(lax.complex(c_re, c_im), 2, 3).reshape(depth, nblk, S5_LANE_GROUPS, S5_STATE, S5_GROUP)
    cm = jnp.concatenate([_block_diag(jnp.real(ct)), -_block_diag(jnp.imag(ct))], axis=-2)
    return (bm.astype(BF16), jnp.real(lam_bar).reshape(depth, 1, -1), jnp.imag(lam_bar).reshape(depth, 1, -1),
            cm.astype(BF16))


def _prepare(norm_g, w_in, rg_conv_w, rg_conv_b, rg_w_a, rg_b_a, rg_w_x, rg_b_x, rg_lambda,
             gdn_conv_w, gdn_a_log, gdn_dt_bias, gdn_norm_g, s5_a_re, s5_a_im, s5_b_re, s5_b_im,
             s5_c_re, s5_c_im, s5_d, s5_log_dt, s5_w_glu, s5_b_glu, w_out, ple_norm_g, ple_w_gate,
             ple_w_proj):
    depth, d_model = norm_g.shape
    o_rgx, o_rgg = 0, RG_W
    o_q = 2 * RG_W
    o_z = o_q + 3 * GDN_W
    o_b = o_z + GDN_W
    o_a = o_b + GDN_H
    o_u = o_a + GDN_H
    o_sg = o_u + S5_W
    cols = lambda c0, width: w_in[:, :, c0:c0 + width]
    row1 = lambda v: v.reshape(depth, 1, -1).astype(F32)
    lane_pad = lambda v: jnp.pad(row1(v), ((0, 0), (0, 0), (0, LANES - GDN_H)))
    zpad = jnp.zeros((depth, d_model, LANES - GDN_H), w_in.dtype)
    nb = RG_W // LANES
    blocks = lambda w: _block_diag(w.reshape(depth, nb, w.shape[1] // nb, w.shape[2], w.shape[3]))
    bm, lam_re, lam_im, cm = _s5_operators(s5_a_re, s5_a_im, s5_b_re, s5_b_im, s5_c_re, s5_c_im, s5_log_dt)
    gdn = dict(
        norm_g=row1(norm_g),
        wqkv=cols(o_q, 3 * GDN_W).astype(BF16),
        wz=(0.5 * cols(o_z, GDN_W)).astype(BF16),
        wab=jnp.concatenate([cols(o_a, GDN_H), zpad, cols(o_b, GDN_H), zpad], axis=2).astype(BF16),
        conv_w=0.5 * gdn_conv_w.astype(F32),
        neg_exp_a=lane_pad(-jnp.exp(gdn_a_log)),
        dt_bias=lane_pad(gdn_dt_bias),
        out_norm_g=row1(gdn_norm_g),
    )
    rg_s5_out = dict(
        norm_g=row1(norm_g),
        w_a=jnp.concatenate([cols(o_rgx, RG_W), 0.5 * cols(o_rgg, RG_W),
                             cols(o_u, S5_W), 0.5 * cols(o_sg, S5_W)], axis=2).astype(BF16),
        conv_w=rg_conv_w.astype(F32),
        conv_b=row1(rg_conv_b),
        wg=jnp.concatenate([blocks(rg_w_a), blocks(rg_w_x)], axis=-1).astype(BF16),
        b_a=row1(rg_b_a),
        b_x=row1(rg_b_x),
        c_rg=row1(-RG_C * jax.nn.softplus(-rg_lambda)),
        bm=bm, lam_re=lam_re, lam_im=lam_im, cm=cm,
        s5_d=row1(s5_d),
        w_glu=s5_w_glu.astype(BF16),
        b_glu=row1(s5_b_glu),
        wo_rg=w_out[:, 0:RG_W].astype(BF16),
        wo_s5=w_out[:, RG_W + GDN_W:].astype(BF16),
        wo_g=w_out[:, RG_W:RG_W + GDN_W].astype(BF16),
        ple_norm_g=row1(ple_norm_g),
        ple_w_gate=ple_w_gate.astype(BF16),
        ple_w_proj=ple_w_proj.astype(BF16),
    )
    return gdn, rg_s5_out


def _layer(h, p, prep, layer, first, final, final_g):
    gdn, rso = prep
    d_model = h.shape[-1]
    n_rows = h.shape[0] * h.shape[1] if first else h.shape[0]
    grid = (n_rows // ROWS,)
    const = lambda arr: _layer_const(arr, layer)

    sel = (jnp.arange(GDN_W)[:, None] // GDN_D == jnp.arange(LANES)[None, :]).astype(BF16)
    gdn_scratch = [pltpu.VMEM((HALO + ROWS, 3 * GDN_W), F32), pltpu.VMEM((ROWS, d_model), BF16),
                   pltpu.VMEM((3 * GDN_H, ROWS, GDN_D), F32),
                   pltpu.VMEM((ROWS, 2 * GDN_W), BF16),
                   pltpu.VMEM((N_SC, ROWS, LANES), F32),
                   pltpu.VMEM((GDN_H, ROWS, GDN_D), F32),
                   pltpu.VMEM((SUBLANES * GDN_H, GDN_D, GDN_D), F32)]
    y_spec, y_shape = _rows(GDN_W), jax.ShapeDtypeStruct((n_rows, GDN_W), BF16)
    if first:
        gdn_scratch.append(pltpu.VMEM((d_model // LANES, ROWS, LANES), F32))
        y_spec, y_shape = [y_spec, _rows(d_model)], [y_shape, jax.ShapeDtypeStruct((n_rows, d_model), F32)]
    gdn_args = [gdn[k] for k in ("norm_g", "wqkv", "wz", "wab", "conv_w", "neg_exp_a", "dt_bias", "out_norm_g")]
    y_g = pl.pallas_call(
        functools.partial(_gdn_kernel, first=first),
        grid=grid,
        in_specs=[_seq_block(d_model) if first else _rows(d_model)] + [const(a) for a in gdn_args]
                 + [_full(sel.shape)],
        out_specs=y_spec,
        out_shape=y_shape,
        scratch_shapes=gdn_scratch,
        compiler_params=_seq_params(),
        name="gdn",
    )(h, *gdn_args, sel)
    if first:
        y_g, h = y_g

    n_state = 2 * S5_BLK_STATE * (S5_W // LANES)
    batch, d_ple = p.shape[1], p.shape[-1]
    scratch = [pltpu.VMEM((HALO + ROWS, RG_W), F32), pltpu.VMEM((ROWS, RG_W), F32),
               pltpu.VMEM((ROWS, RG_W), F32), pltpu.VMEM((SUBLANES, RG_W), F32),
               pltpu.VMEM((ROWS, n_state), F32), pltpu.VMEM((SUBLANES, n_state), F32),
               pltpu.VMEM((d_ple // LANES, ROWS, LANES), F32)]
    if final:
        scratch.append(pltpu.VMEM((d_model // LANES, ROWS, LANES), F32))
        o_spec, o_shape = _seq_block(d_model), jax.ShapeDtypeStruct((batch, n_rows // batch, d_model), F32)
    else:
        o_spec, o_shape = _rows(d_model), jax.ShapeDtypeStruct((n_rows, d_model), F32)
    head = [rso[k] for k in ("norm_g", "w_a", "conv_w", "conv_b", "wg", "b_a", "b_x", "c_rg",
                             "bm", "lam_re", "lam_im", "cm", "s5_d", "w_glu", "b_glu")]
    tail = [rso[k] for k in ("wo_rg", "wo_s5", "wo_g", "ple_norm_g", "ple_w_gate", "ple_w_proj")]
    p_spec = pl.BlockSpec((None, SUBLANES, TILE_T, d_ple), lambda i: (layer, 0, i, 0))
    return pl.pallas_call(
        functools.partial(_rg_s5_out_kernel, final=final),
        grid=grid,
        in_specs=[_rows(d_model)] + [const(a) for a in head] + [_rows(GDN_W), p_spec]
                 + [const(a) for a in tail] + [_full((1, d_model))],
        out_specs=o_spec,
        out_shape=o_shape,
        scratch_shapes=scratch,
        compiler_params=_seq_params(),
        name="rg_s5_out",
    )(h, *head, y_g, p, *tail, final_g.reshape(1, -1).astype(F32))


def kernel(x, p, norm_g, w_in, rg_conv_w, rg_conv_b, rg_w_a, rg_b_a, rg_w_x, rg_b_x, rg_lambda, gdn_conv_w, gdn_a_log, gdn_dt_bias, gdn_norm_g, s5_a_re, s5_a_im, s5_b_re, s5_b_im, s5_c_re, s5_c_im, s5_d, s5_log_dt, s5_w_glu, s5_b_glu, w_out, ple_norm_g, ple_w_gate, ple_w_proj, final_norm_g):
    batch, seq, d_model = x.shape
    depth = p.shape[0]
    assert batch == SUBLANES and seq % TILE_T == 0 and depth >= 1
    assert x.dtype == F32
    prep = _prepare(norm_g, w_in, rg_conv_w, rg_conv_b, rg_w_a, rg_b_a, rg_w_x, rg_b_x, rg_lambda,
                    gdn_conv_w, gdn_a_log, gdn_dt_bias, gdn_norm_g, s5_a_re, s5_a_im, s5_b_re, s5_b_im,
                    s5_c_re, s5_c_im, s5_d, s5_log_dt, s5_w_glu, s5_b_glu, w_out, ple_norm_g, ple_w_gate,
                    ple_w_proj)
    h = x
    for i in range(depth):
        h = _layer(h, p, prep, i, i == 0, i == depth - 1, final_norm_g)
    return h
```

```python
import functools

import jax
import jax.numpy as jnp
from jax import lax
from jax.experimental import pallas as pl
from jax.experimental.pallas import tpu as pltpu

F32 = jnp.float32
BF16 = jnp.bfloat16

NORM_EPS = 1e-6
RG_C = 8.0
CONV_K = 4

SUBLANES = 8
LANES = 128
TILE_T = 64
ROWS = TILE_T * SUBLANES
HALO = (CONV_K - 1) * SUBLANES

RG_W = 512
S5_W = 512
S5_GROUP = 16
S5_STATE = 64
S5_LANE_GROUPS = LANES // S5_GROUP
S5_BLK_STATE = S5_LANE_GROUPS * S5_STATE
GDN_H = 8
GDN_D = 128
GDN_W = GDN_H * GDN_D

VMEM_LIMIT = 56 * 1024 * 1024


def _rms(x, g):
    ms = jnp.mean(x * x, axis=-1, keepdims=True)
    return x * lax.rsqrt(ms + NORM_EPS) * g


def _sigmoid(x):
    return 1.0 / (1.0 + jnp.exp(-x))


def _silu_half(hx):
    return hx + hx * jnp.tanh(hx)


def _softplus(x):
    return jnp.maximum(x, 0.0) + jnp.log1p(jnp.exp(-jnp.abs(x)))


def _expm1(x):
    u = jnp.exp(x)
    um1 = u - 1.0
    regular = jnp.logical_and(um1 != 0.0, um1 != -1.0)
    v = um1 * x / jnp.log(jnp.where(regular, u, 2.0))
    return jnp.where(um1 == 0.0, x, jnp.where(um1 == -1.0, -1.0, v))


def _dot(a, b):
    return jnp.dot(a.astype(BF16), b.astype(BF16), preferred_element_type=F32)


def _dot_nt(a, b):
    return lax.dot_general(a.astype(BF16), b.astype(BF16), (((1,), (1,)), ((), ())),
                           preferred_element_type=F32)


def _dot_tn(a, b):
    return lax.dot_general(a.astype(BF16), b.astype(BF16), (((0,), (0,)), ((), ())),
                           preferred_element_type=F32)


def _causal_conv(x, buf, cw):
    buf[HALO:HALO + ROWS, :] = x
    y = cw[CONV_K - 1:CONV_K, :] * x
    for k in range(CONV_K - 1):
        off = HALO - (CONV_K - 1 - k) * SUBLANES
        y = y + cw[k:k + 1, :] * buf[off:off + ROWS, :]
    buf[0:HALO, :] = buf[ROWS:ROWS + HALO, :]
    return y


def _to_time_major(src_ref, chunk_buf):
    n_chunks = src_ref.shape[-1] // LANES
    for b in range(SUBLANES):
        for c in range(n_chunks):
            chunk_buf[c, pl.ds(b, TILE_T, stride=SUBLANES), :] = src_ref[b, :, c * LANES:(c + 1) * LANES]
    return jnp.concatenate([chunk_buf[c] for c in range(n_chunks)], axis=1)


def _from_time_major(val, chunk_buf, dst_ref):
    n_chunks = val.shape[-1] // LANES
    for c in range(n_chunks):
        chunk_buf[c] = val[:, c * LANES:(c + 1) * LANES]
    for b in range(SUBLANES):
        for c in range(n_chunks):
            dst_ref[b, :, c * LANES:(c + 1) * LANES] = chunk_buf[c, pl.ds(b, TILE_T, stride=SUBLANES), :]


def _rg_s5_out_kernel(h_ref, ng_ref, w_ref, cw_ref, cb_ref, wg_ref, ba_ref, bx_ref, c_ref,
                      bm_ref, lr_ref, li_ref, cm_ref, d_ref, wglu_ref, bglu_ref,
                      yg_ref, p_ref, wo_rg_ref, wo_s5_ref, wo_g_ref, png_ref, wgate_ref, wproj_ref, fg_ref,
                      o_ref,
                      xbuf, a_buf, i_buf, hcar, st_buf, scar, p_buf, *rest, final):
    @pl.when(pl.program_id(0) == 0)
    def _():
        xbuf[0:HALO, :] = jnp.zeros((HALO, RG_W), F32)
        hcar[...] = jnp.zeros_like(hcar)
        scar[...] = jnp.zeros_like(scar)

    h = h_ref[...]
    hn = _rms(h, ng_ref[...]).astype(BF16)
    w_cols = lambda c0, width: w_ref[:, c0:c0 + width]
    nblk = S5_W // LANES
    sw = 2 * S5_BLK_STATE

    def s5_scan(j):
        c_re = j * sw
        c_im = j * sw + S5_BLK_STATE
        lr = jnp.broadcast_to(lr_ref[:, j * S5_BLK_STATE:(j + 1) * S5_BLK_STATE], (SUBLANES, S5_BLK_STATE))
        li = jnp.broadcast_to(li_ref[:, j * S5_BLK_STATE:(j + 1) * S5_BLK_STATE], (SUBLANES, S5_BLK_STATE))

        def s5_step(t, carry):
            sr, si = carry
            r0 = pl.multiple_of(t * SUBLANES, SUBLANES)
            nr = lr * sr - li * si + st_buf[pl.ds(r0, SUBLANES), c_re:c_re + S5_BLK_STATE]
            ni = lr * si + li * sr + st_buf[pl.ds(r0, SUBLANES), c_im:c_im + S5_BLK_STATE]
            st_buf[pl.ds(r0, SUBLANES), c_re:c_re + S5_BLK_STATE] = nr
            st_buf[pl.ds(r0, SUBLANES), c_im:c_im + S5_BLK_STATE] = ni
            return nr, ni

        sr, si = lax.fori_loop(
            0, TILE_T, s5_step,
            (scar[:, c_re:c_re + S5_BLK_STATE], scar[:, c_im:c_im + S5_BLK_STATE]), unroll=True)
        scar[:, c_re:c_re + S5_BLK_STATE] = sr
        scar[:, c_im:c_im + S5_BLK_STATE] = si

    def s5_readout(j):
        return _dot(st_buf[:, j * sw:(j + 1) * sw], cm_ref[j])

    rg_x = jnp.dot(hn, w_cols(0, RG_W), preferred_element_type=F32)
    s5_u = jnp.dot(hn, w_cols(2 * RG_W, S5_W), preferred_element_type=F32)
    for j in range(nblk):
        st_buf[:, j * sw:(j + 1) * sw] = _dot(s5_u[:, j * LANES:(j + 1) * LANES], bm_ref[j])

    xr = _causal_conv(rg_x, xbuf, cw_ref[...]) + cb_ref[...]
    pre_a, pre_x = [], []
    for j in range(RG_W // LANES):
        g = _dot(xr[:, j * LANES:(j + 1) * LANES], wg_ref[j])
        pre_a.append(g[:, :LANES])
        pre_x.append(g[:, LANES:])

    rg_gate = jnp.dot(hn, w_cols(RG_W, RG_W), preferred_element_type=F32)
    s5_scan(0)
    s5_gate = jnp.dot(hn, w_cols(2 * RG_W + S5_W, S5_W), preferred_element_type=F32)
    s5_scan(1)
    acc = h + jnp.dot(yg_ref[...], wo_g_ref[...], preferred_element_type=F32)

    r = _sigmoid(jnp.concatenate(pre_a, axis=1) + ba_ref[...])
    gi = _sigmoid(jnp.concatenate(pre_x, axis=1) + bx_ref[...])
    log_a = c_ref[...] * r
    a_buf[...] = jnp.exp(log_a)
    i_buf[...] = jnp.sqrt(-_expm1(2.0 * log_a)) * (gi * xr)
    ys = [s5_readout(0), s5_readout(1)]
    s5_scan(2)
    s5_scan(3)

    def rg_step(t, hc):
        r0 = pl.multiple_of(t * SUBLANES, SUBLANES)
        hc = a_buf[pl.ds(r0, SUBLANES), :] * hc + i_buf[pl.ds(r0, SUBLANES), :]
        i_buf[pl.ds(r0, SUBLANES), :] = hc
        return hc

    hcar[...] = lax.fori_loop(0, TILE_T, rg_step, hcar[...], unroll=True)
    y_rg = (i_buf[...] * _silu_half(rg_gate)).astype(BF16)
    acc = acc + jnp.dot(y_rg, wo_rg_ref[...], preferred_element_type=F32)

    ys += [s5_readout(2), s5_readout(3)]
    y5 = jnp.concatenate(ys, axis=1) + d_ref[...] * s5_u
    z5 = jax.nn.gelu(y5)
    y5 = z5 * _sigmoid(_dot(z5, wglu_ref[...]) + bglu_ref[...])
    y_s5 = (y5 * _silu_half(s5_gate)).astype(BF16)

    h = acc + jnp.dot(y_s5, wo_s5_ref[...], preferred_element_type=F32)
    gate = _sigmoid(_dot(_rms(h, png_ref[...]), wgate_ref[...]))
    h = h + gate * _dot(_to_time_major(p_ref, p_buf), wproj_ref[...])
    if final:
        _from_time_major(_rms(h, fg_ref[...]), rest[0], o_ref)
    else:
        o_ref[...] = h


N_SC = 8
SC_GC, SC_BRK, SC_RQ, SC_QEG, SC_KDL, SC_RK, SC_BETA, SC_TWC = range(N_SC)
SC_COLS = (SC_GC, SC_BRK, SC_RQ, SC_QEG, SC_KDL)
SC_ROWS = (SC_GC, SC_RK, SC_BETA, SC_TWC)
CONV_RB = 32
GDN_PAR = 4


def _gdn_kernel(h_ref, ng_ref, wqkv_ref, wz_ref, wab_ref, cw_ref, nea_ref, dtb_ref, gn_ref, sel_ref,
                y_ref, *rest, first):
    if first:
        htm_ref, cbuf, hn_buf, qkv_s, sq_s, sc_s, o_s, st_ref, chunk_buf = rest
    else:
        cbuf, hn_buf, qkv_s, sq_s, sc_s, o_s, st_ref = rest

    @pl.when(pl.program_id(0) == 0)
    def _():
        cbuf[0:HALO, :] = jnp.zeros((HALO, 3 * GDN_W), F32)
        st_ref[...] = jnp.zeros_like(st_ref)

    if first:
        h = _to_time_major(h_ref, chunk_buf)
        htm_ref[...] = h
    else:
        h = h_ref[...]
    hn_buf[...] = _rms(h, ng_ref[...]).astype(BF16)

    ab = jnp.dot(hn_buf[...], wab_ref[...], preferred_element_type=F32)
    g = nea_ref[...] * _softplus(ab[:, :LANES] + dtb_ref[...])
    beta = _sigmoid(ab[:, LANES:])
    gc = g
    d = 1
    while d < TILE_T:
        sh = d * SUBLANES
        gc = gc + jnp.concatenate([jnp.zeros((sh, LANES), F32), gc[:ROWS - sh, :]], axis=0)
        d *= 2
    g_last = jnp.broadcast_to(gc[ROWS - SUBLANES:, :][None], (TILE_T, SUBLANES, LANES)).reshape(ROWS, LANES)

    def project(c3):
        cbuf[HALO:HALO + ROWS, c3 * GDN_W:(c3 + 1) * GDN_W] = jnp.dot(
            hn_buf[...], wqkv_ref[:, c3 * GDN_W:(c3 + 1) * GDN_W], preferred_element_type=F32)

    def conv_group(c3):
        for c in range(c3 * GDN_H, (c3 + 1) * GDN_H):
            cols = slice(c * GDN_D, (c + 1) * GDN_D)
            taps = [cw_ref[k:k + 1, cols] for k in range(CONV_K)]
            for r0 in range(0, ROWS, CONV_RB):
                y = taps[CONV_K - 1] * cbuf[r0 + HALO:r0 + HALO + CONV_RB, cols]
                for k in range(CONV_K - 1):
                    off = r0 + HALO - (CONV_K - 1 - k) * SUBLANES
                    y = y + taps[k] * cbuf[off:off + CONV_RB, cols]
                act = _silu_half(y)
                qkv_s[c, r0:r0 + CONV_RB, :] = act
                if c3 < 2:
                    sq_s[r0:r0 + CONV_RB, cols] = (act * act).astype(BF16)

    project(0)
    project(1)
    conv_group(0)
    project(2)
    conv_group(1)
    z = jnp.dot(hn_buf[...], wz_ref[...], preferred_element_type=F32)
    conv_group(2)
    halo = cbuf[ROWS:ROWS + HALO, :]
    cbuf[0:HALO, :] = halo

    ss_q = jnp.dot(sq_s[:, 0:GDN_W], sel_ref[...], preferred_element_type=F32)
    ss_k = jnp.dot(sq_s[:, GDN_W:2 * GDN_W], sel_ref[...], preferred_element_type=F32)
    rq = lax.rsqrt(ss_q + NORM_EPS) * (GDN_D ** -0.5)
    rk = lax.rsqrt(ss_k + NORM_EPS)
    egc = jnp.exp(gc)
    sc_s[SC_GC] = gc
    sc_s[SC_BRK] = beta * rk
    sc_s[SC_RQ] = rq
    sc_s[SC_QEG] = rq * egc
    sc_s[SC_KDL] = rk * jnp.exp(g_last - gc)
    sc_s[SC_RK] = rk
    sc_s[SC_BETA] = beta
    sc_s[SC_TWC] = beta * rk * egc

    row = lax.broadcasted_iota(jnp.int32, (TILE_T, TILE_T), 0)
    col = lax.broadcasted_iota(jnp.int32, (TILE_T, TILE_T), 1)
    causal = row >= col
    strict = row > col
    eye = (row == col).astype(F32)

    def per_group(i, carry):
        chains = [(i * GDN_PAR + s, h) for s in range(GDN_PAR) for h in range(GDN_H)]
        colv, rowv = [], []
        for s in range(GDN_PAR):
            rows = pl.ds(i * GDN_PAR + s, TILE_T, stride=SUBLANES)
            colv.append({n: sc_s[n, rows, :] for n in SC_COLS})
            pad = jnp.zeros((TILE_T, LANES), F32)
            rowv.append({n: jnp.concatenate([sc_s[n, rows, :], pad], axis=0).T for n in SC_ROWS})
        ck = lambda s, n, h: colv[s][n][:, h:h + 1]
        rw = lambda s, n, h: rowv[s][n][h:h + 1, 0:TILE_T]

        q_b, k_b, v_b = [], [], []
        for b, h in chains:
            rows = pl.ds(b, TILE_T, stride=SUBLANES)
            q_b.append(qkv_s[h, rows, :].astype(BF16))
            k_b.append(qkv_s[GDN_H + h, rows, :].astype(BF16))
            v_b.append(qkv_s[2 * GDN_H + h, rows, :].astype(BF16))
        n_ch = len(chains)
        kk = [_dot_nt(jnp.concatenate([k_b[c], q_b[c]], axis=0), k_b[c]) for c in range(n_ch)]
        a_qk, neg = [], []
        for c, (b, h) in enumerate(chains):
            s = c // GDN_H
            diff = ck(s, SC_GC, h) - rw(s, SC_GC, h)
            dk = jnp.where(causal, jnp.exp(jnp.where(causal, diff, 0.0)), 0.0) * rw(s, SC_RK, h)
            neg.append(jnp.where(strict, -(kk[c][:TILE_T] * ck(s, SC_BRK, h)) * dk, 0.0))
            a_qk.append(kk[c][TILE_T:] * ck(s, SC_RQ, h) * dk)
        tinv = [eye + n for n in neg]
        pw = [_dot(n, n) for n in neg]
        span = 2
        while span < TILE_T:
            span *= 2
            if span < TILE_T:
                res = [_dot(jnp.concatenate([t, p], axis=0), p) for t, p in zip(tinv, pw)]
                tinv = [t + r[:TILE_T] for t, r in zip(tinv, res)]
                pw = [r[TILE_T:] for r in res]
            else:
                tinv = [t + _dot(t, p) for t, p in zip(tinv, pw)]
        u, w = [], []
        for c, (b, h) in enumerate(chains):
            s = c // GDN_H
            u.append(_dot(tinv[c] * rw(s, SC_BETA, h), v_b[c]))
            w.append(_dot(tinv[c] * rw(s, SC_TWC, h), k_b[c]))
        state = [st_ref[b * GDN_H + h] for b, h in chains]
        ws_qs = [_dot(jnp.concatenate([w[c].astype(BF16), q_b[c]], axis=0), state[c]) for c in range(n_ch)]
        v_new = [u[c] - ws_qs[c][:TILE_T] for c in range(n_ch)]
        for c, (b, h) in enumerate(chains):
            s = c // GDN_H
            o_s[h, pl.ds(b, TILE_T, stride=SUBLANES), :] = (
                ws_qs[c][TILE_T:] * ck(s, SC_QEG, h) + _dot(a_qk[c], v_new[c]))
        for c, (b, h) in enumerate(chains):
            s = c // GDN_H
            e_last = jnp.exp(rowv[s][SC_GC][h:h + 1, TILE_T - 1:TILE_T])
            st_ref[b * GDN_H + h] = state[c] * e_last + _dot_tn(k_b[c], v_new[c] * ck(s, SC_KDL, h))
        return carry

    lax.fori_loop(0, SUBLANES // GDN_PAR, per_group, 0)

    for h in range(GDN_H):
        sl = slice(h * GDN_D, (h + 1) * GDN_D)
        y_ref[:, sl] = (_rms(o_s[h], gn_ref[...]) * _silu_half(z[:, sl])).astype(y_ref.dtype)


def _layer_const(arr, layer):
    nd = arr.ndim - 1
    return pl.BlockSpec((None,) + arr.shape[1:], lambda i: (layer,) + (0,) * nd, pipeline_mode=pl.Buffered(1))


def _full(shape):
    nd = len(shape)
    return pl.BlockSpec(shape, lambda i: (0,) * nd, pipeline_mode=pl.Buffered(1))


def _rows(width):
    return pl.BlockSpec((ROWS, width), lambda i: (i, 0))


def _seq_block(width):
    return pl.BlockSpec((SUBLANES, TILE_T, width), lambda i: (0, i, 0))


def _seq_params():
    return pltpu.CompilerParams(dimension_semantics=("arbitrary",), vmem_limit_bytes=VMEM_LIMIT)


def _block_diag(blocks):
    n, r, c = blocks.shape[-3:]
    eye = jnp.eye(n, dtype=blocks.dtype)
    return jnp.einsum('...grc,gh->...grhc', blocks, eye).reshape(blocks.shape[:-3] + (n * r, n * c))


def _s5_operators(a_re, a_im, b_re, b_im, c_re, c_im, log_dt):
    depth = a_re.shape[0]
    lam = lax.complex(a_re, a_im)
    dt = jnp.exp(log_dt)[..., None]
    lam_bar = jnp.exp(lam * dt)
    b_bar = ((lam_bar - 1.0) / lam)[..., None] * lax.complex(b_re, b_im)
    nblk = S5_W // LANES
    bt = jnp.swapaxes(b_bar, 2, 3).reshape(depth, nblk, S5_LANE_GROUPS, S5_GROUP, S5_STATE)
    bm = jnp.concatenate([_block_diag(jnp.real(bt)), _block_diag(jnp.imag(bt))], axis=-1)
    ct = jnp.swapaxes(lax.complex(c_re, c_im), 2, 3).reshape(depth, nblk, S5_LANE_GROUPS, S5_STATE, S5_GROUP)
    cm = jnp.concatenate([_block_diag(jnp.real(ct)), -_block_diag(jnp.imag(ct))], axis=-2)
    return (bm.astype(BF16), jnp.real(lam_bar).reshape(depth, 1, -1), jnp.imag(lam_bar).reshape(depth, 1, -1),
            cm.astype(BF16))


def _prepare(norm_g, w_in, rg_conv_w, rg_conv_b, rg_w_a, rg_b_a, rg_w_x, rg_b_x, rg_lambda,
             gdn_conv_w, gdn_a_log, gdn_dt_bias, gdn_norm_g, s5_a_re, s5_a_im, s5_b_re, s5_b_im,
             s5_c_re, s5_c_im, s5_d, s5_log_dt, s5_w_glu, s5_b_glu, w_out, ple_norm_g, ple_w_gate,
             ple_w_proj):
    depth, d_model = norm_g.shape
    o_rgx, o_rgg = 0, RG_W
    o_q = 2 * RG_W
    o_z = o_q + 3 * GDN_W
    o_b = o_z + GDN_W
    o_a = o_b + GDN_H
    o_u = o_a + GDN_H
    o_sg = o_u + S5_W
    cols = lambda c0, width: w_in[:, :, c0:c0 + width]
    row1 = lambda v: v.reshape(depth, 1, -1).astype(F32)
    lane_pad = lambda v: jnp.pad(row1(v), ((0, 0), (0, 0), (0, LANES - GDN_H)))
    zpad = jnp.zeros((depth, d_model, LANES - GDN_H), w_in.dtype)
    nb = RG_W // LANES
    blocks = lambda w: _block_diag(w.reshape(depth, nb, w.shape[1] // nb, w.shape[2], w.shape[3]))
    bm, lam_re, lam_im, cm = _s5_operators(s5_a_re, s5_a_im, s5_b_re, s5_b_im, s5_c_re, s5_c_im, s5_log_dt)
    gdn = dict(
        norm_g=row1(norm_g),
        wqkv=cols(o_q, 3 * GDN_W).astype(BF16),
        wz=(0.5 * cols(o_z, GDN_W)).astype(BF16),
        wab=jnp.concatenate([cols(o_a, GDN_H), zpad, cols(o_b, GDN_H), zpad], axis=2).astype(BF16),
        conv_w=0.5 * gdn_conv_w.astype(F32),
        neg_exp_a=lane_pad(-jnp.exp(gdn_a_log)),
        dt_bias=lane_pad(gdn_dt_bias),
        out_norm_g=row1(gdn_norm_g),
    )
    rg_s5_out = dict(
        norm_g=row1(norm_g),
        w_a=jnp.concatenate([cols(o_rgx, RG_W), 0.5 * cols(o_rgg, RG_W),
                             cols(o_u, S5_W), 0.5 * cols(o_sg, S5_W)], axis=2).astype(BF16),
        conv_w=rg_conv_w.astype(F32),
        conv_b=row1(rg_conv_b),
        wg=jnp.concatenate([blocks(rg_w_a), blocks(rg_w_x)], axis=-1).astype(BF16),
        b_a=row1(rg_b_a),
        b_x=row1(rg_b_x),
        c_rg=row1(-RG_C * jax.nn.softplus(-rg_lambda)),
        bm=bm, lam_re=lam_re, lam_im=lam_im, cm=cm,
        s5_d=row1(s5_d),
        w_glu=s5_w_glu.astype(BF16),
        b_glu=row1(s5_b_glu),
        wo_rg=w_out[:, 0:RG_W].astype(BF16),
        wo_s5=w_out[:, RG_W + GDN_W:].astype(BF16),
        wo_g=w_out[:, RG_W:RG_W + GDN_W].astype(BF16),
        ple_norm_g=row1(ple_norm_g),
        ple_w_gate=ple_w_gate.astype(BF16),
        ple_w_proj=ple_w_proj.astype(BF16),
    )
    return gdn, rg_s5_out


def _layer(h, p, prep, layer, first, final, final_g):
    gdn, rso = prep
    d_model = h.shape[-1]
    n_rows = h.shape[0] * h.shape[1] if first else h.shape[0]
    grid = (n_rows // ROWS,)
    const = lambda arr: _layer_const(arr, layer)

    sel = (jnp.arange(GDN_W)[:, None] // GDN_D == jnp.arange(LANES)[None, :]).astype(BF16)
    gdn_scratch = [pltpu.VMEM((HALO + ROWS, 3 * GDN_W), F32), pltpu.VMEM((ROWS, d_model), BF16),
                   pltpu.VMEM((3 * GDN_H, ROWS, GDN_D), F32),
                   pltpu.VMEM((ROWS, 2 * GDN_W), BF16),
                   pltpu.VMEM((N_SC, ROWS, LANES), F32),
                   pltpu.VMEM((GDN_H, ROWS, GDN_D), F32),
                   pltpu.VMEM((SUBLANES * GDN_H, GDN_D, GDN_D), F32)]
    y_spec, y_shape = _rows(GDN_W), jax.ShapeDtypeStruct((n_rows, GDN_W), BF16)
    if first:
        gdn_scratch.append(pltpu.VMEM((d_model // LANES, ROWS, LANES), F32))
        y_spec, y_shape = [y_spec, _rows(d_model)], [y_shape, jax.ShapeDtypeStruct((n_rows, d_model), F32)]
    gdn_args = [gdn[k] for k in ("norm_g", "wqkv", "wz", "wab", "conv_w", "neg_exp_a", "dt_bias", "out_norm_g")]
    y_g = pl.pallas_call(
        functools.partial(_gdn_kernel, first=first),
        grid=grid,
        in_specs=[_seq_block(d_model) if first else _rows(d_model)] + [const(a) for a in gdn_args]
                 + [_full(sel.shape)],
        out_specs=y_spec,
        out_shape=y_shape,
        scratch_shapes=gdn_scratch,
        compiler_params=_seq_params(),
        name="gdn",
    )(h, *gdn_args, sel)
    if first:
        y_g, h = y_g

    n_state = 2 * S5_BLK_STATE * (S5_W // LANES)
    batch, d_ple = p.shape[1], p.shape[-1]
    scratch = [pltpu.VMEM((HALO + ROWS, RG_W), F32), pltpu.VMEM((ROWS, RG_W), F32),
               pltpu.VMEM((ROWS, RG_W), F32), pltpu.VMEM((SUBLANES, RG_W), F32),
               pltpu.VMEM((ROWS, n_state), F32), pltpu.VMEM((SUBLANES, n_state), F32),
               pltpu.VMEM((d_ple // LANES, ROWS, LANES), F32)]
    if final:
        scratch.append(pltpu.VMEM((d_model // LANES, ROWS, LANES), F32))
        o_spec, o_shape = _seq_block(d_model), jax.ShapeDtypeStruct((batch, n_rows // batch, d_model), F32)
    else:
        o_spec, o_shape = _rows(d_model), jax.ShapeDtypeStruct((n_rows, d_model), F32)
    head = [rso[k] for k in ("norm_g", "w_a", "conv_w", "conv_b", "wg", "b_a", "b_x", "c_rg",
                             "bm", "lam_re", "lam_im", "cm", "s5_d", "w_glu", "b_glu")]
    tail = [rso[k] for k in ("wo_rg", "wo_s5", "wo_g", "ple_norm_g", "ple_w_gate", "ple_w_proj")]
    p_spec = pl.BlockSpec((None, SUBLANES, TILE_T, d_ple), lambda i: (layer, 0, i, 0))
    return pl.pallas_call(
        functools.partial(_rg_s5_out_kernel, final=final),
        grid=grid,
        in_specs=[_rows(d_model)] + [const(a) for a in head] + [_rows(GDN_W), p_spec]
                 + [const(a) for a in tail] + [_full((1, d_model))],
        out_specs=o_spec,
        out_shape=o_shape,
        scratch_shapes=scratch,
        compiler_params=_seq_params(),
        name="rg_s5_out",
    )(h, *head, y_g, p, *tail, final_g.reshape(1, -1).astype(F32))


def kernel(x, p, norm_g, w_in, rg_conv_w, rg_conv_b, rg_w_a, rg_b_a, rg_w_x, rg_b_x, rg_lambda, gdn_conv_w, gdn_a_log, gdn_dt_bias, gdn_norm_g, s5_a_re, s5_a_im, s5_b_re, s5_b_im, s5_c_re, s5_c_im, s5_d, s5_log_dt, s5_w_glu, s5_b_glu, w_out, ple_norm_g, ple_w_gate, ple_w_proj, final_norm_g):
    batch, seq, d_model = x.shape
    depth = p.shape[0]
    assert batch == SUBLANES and seq % TILE_T == 0 and depth >= 1
    assert x.dtype == F32
    prep = _prepare(norm_g, w_in, rg_conv_w, rg_conv_b, rg_w_a, rg_b_a, rg_w_x, rg_b_x, rg_lambda,
                    gdn_conv_w, gdn_a_log, gdn_dt_bias, gdn_norm_g, s5_a_re, s5_a_im, s5_b_re, s5_b_im,
                    s5_c_re, s5_c_im, s5_d, s5_log_dt, s5_w_glu, s5_b_glu, w_out, ple_norm_g, ple_w_gate,
                    ple_w_proj)
    h = x
    for i in range(depth):
        h = _layer(h, p, prep, i, i == 0, i == depth - 1, final_norm_g)
    return h
```

```python
import functools

import jax
import jax.numpy as jnp
from jax import lax
from jax.experimental import pallas as pl
from jax.experimental.pallas import tpu as pltpu

F32 = jnp.float32
BF16 = jnp.bfloat16

NORM_EPS = 1e-6
RG_C = 8.0
CONV_K = 4

SUBLANES = 8
LANES = 128
TILE_T = 64
ROWS = TILE_T * SUBLANES
HALO = (CONV_K - 1) * SUBLANES

RG_W = 512
S5_W = 512
S5_GROUP = 16
S5_STATE = 64
S5_LANE_GROUPS = LANES // S5_GROUP
S5_BLK_STATE = S5_LANE_GROUPS * S5_STATE
GDN_H = 8
GDN_D = 128
GDN_W = GDN_H * GDN_D

VMEM_LIMIT = 56 * 1024 * 1024


def _rms(x, g):
    ms = jnp.mean(x * x, axis=-1, keepdims=True)
    return x * lax.rsqrt(ms + NORM_EPS) * g


def _sigmoid(x):
    return 1.0 / (1.0 + jnp.exp(-x))


def _silu_half(hx):
    return hx + hx * jnp.tanh(hx)


def _softplus(x):
    return jnp.maximum(x, 0.0) + jnp.log1p(jnp.exp(-jnp.abs(x)))


def _neg_expm1_2x(x):
    t = jnp.tanh(x)
    return -2.0 * t / (1.0 - t)


def _dot(a, b):
    return jnp.dot(a.astype(BF16), b.astype(BF16), preferred_element_type=F32)


def _dot_nt(a, b):
    return lax.dot_general(a.astype(BF16), b.astype(BF16), (((1,), (1,)), ((), ())),
                           preferred_element_type=F32)


def _dot_tn(a, b):
    return lax.dot_general(a.astype(BF16), b.astype(BF16), (((0,), (0,)), ((), ())),
                           preferred_element_type=F32)


def _causal_conv(x, buf, cw):
    buf[HALO:HALO + ROWS, :] = x
    y = cw[CONV_K - 1:CONV_K, :] * x
    for k in range(CONV_K - 1):
        off = HALO - (CONV_K - 1 - k) * SUBLANES
        y = y + cw[k:k + 1, :] * buf[off:off + ROWS, :]
    buf[0:HALO, :] = buf[ROWS:ROWS + HALO, :]
    return y


def _to_time_major(src_ref, chunk_buf):
    n_chunks = src_ref.shape[-1] // LANES
    for b in range(SUBLANES):
        for c in range(n_chunks):
            chunk_buf[c, pl.ds(b, TILE_T, stride=SUBLANES), :] = src_ref[b, :, c * LANES:(c + 1) * LANES]
    return jnp.concatenate([chunk_buf[c] for c in range(n_chunks)], axis=1)


def _from_time_major(val, chunk_buf, dst_ref):
    n_chunks = val.shape[-1] // LANES
    for c in range(n_chunks):
        chunk_buf[c] = val[:, c * LANES:(c + 1) * LANES]
    for b in range(SUBLANES):
        for c in range(n_chunks):
            dst_ref[b, :, c * LANES:(c + 1) * LANES] = chunk_buf[c, pl.ds(b, TILE_T, stride=SUBLANES), :]


def _rg_s5_out_kernel(h_ref, ng_ref, w_ref, cw_ref, cb_ref, wg_ref, ba_ref, bx_ref, c_ref,
                      bm_ref, lr_ref, li_ref, cm_ref, d_ref, wglu_ref, bglu_ref,
                      yg_ref, p_ref, wo_rg_ref, wo_s5_ref, wo_g_ref, png_ref, wgate_ref, wproj_ref, fg_ref,
                      o_ref,
                      xbuf, a_buf, i_buf, hcar, st_buf, scar, p_buf, *rest, final):
    @pl.when(pl.program_id(0) == 0)
    def _():
        xbuf[0:HALO, :] = jnp.zeros((HALO, RG_W), F32)
        hcar[...] = jnp.zeros_like(hcar)
        scar[...] = jnp.zeros_like(scar)

    h = h_ref[...]
    hn = _rms(h, ng_ref[...]).astype(BF16)
    w_cols = lambda c0, width: w_ref[:, c0:c0 + width]
    nblk = S5_W // LANES
    sw = 2 * S5_BLK_STATE

    def s5_scan(j):
        c_re = j * sw
        c_im = j * sw + S5_BLK_STATE
        lr = jnp.broadcast_to(lr_ref[:, j * S5_BLK_STATE:(j + 1) * S5_BLK_STATE], (SUBLANES, S5_BLK_STATE))
        li = jnp.broadcast_to(li_ref[:, j * S5_BLK_STATE:(j + 1) * S5_BLK_STATE], (SUBLANES, S5_BLK_STATE))

        def s5_step(t, carry):
            sr, si = carry
            r0 = pl.multiple_of(t * SUBLANES, SUBLANES)
            nr = lr * sr - li * si + st_buf[pl.ds(r0, SUBLANES), c_re:c_re + S5_BLK_STATE]
            ni = lr * si + li * sr + st_buf[pl.ds(r0, SUBLANES), c_im:c_im + S5_BLK_STATE]
            st_buf[pl.ds(r0, SUBLANES), c_re:c_re + S5_BLK_STATE] = nr
            st_buf[pl.ds(r0, SUBLANES), c_im:c_im + S5_BLK_STATE] = ni
            return nr, ni

        sr, si = lax.fori_loop(
            0, TILE_T, s5_step,
            (scar[:, c_re:c_re + S5_BLK_STATE], scar[:, c_im:c_im + S5_BLK_STATE]), unroll=True)
        scar[:, c_re:c_re + S5_BLK_STATE] = sr
        scar[:, c_im:c_im + S5_BLK_STATE] = si

    def s5_readout(j):
        return _dot(st_buf[:, j * sw:(j + 1) * sw], cm_ref[j])

    rg_x = jnp.dot(hn, w_cols(0, RG_W), preferred_element_type=F32)
    s5_u = jnp.dot(hn, w_cols(2 * RG_W, S5_W), preferred_element_type=F32)
    for j in range(nblk):
        st_buf[:, j * sw:(j + 1) * sw] = _dot(s5_u[:, j * LANES:(j + 1) * LANES], bm_ref[j])

    xr = _causal_conv(rg_x, xbuf, cw_ref[...]) + cb_ref[...]
    pre_a, pre_x = [], []
    for j in range(RG_W // LANES):
        g = _dot(xr[:, j * LANES:(j + 1) * LANES], wg_ref[j])
        pre_a.append(g[:, :LANES])
        pre_x.append(g[:, LANES:])

    rg_gate = jnp.dot(hn, w_cols(RG_W, RG_W), preferred_element_type=F32)
    s5_scan(0)
    s5_gate = jnp.dot(hn, w_cols(2 * RG_W + S5_W, S5_W), preferred_element_type=F32)
    s5_scan(1)
    acc = h + jnp.dot(yg_ref[...], wo_g_ref[...], preferred_element_type=F32)

    r = _sigmoid(jnp.concatenate(pre_a, axis=1) + ba_ref[...])
    gi = _sigmoid(jnp.concatenate(pre_x, axis=1) + bx_ref[...])
    log_a = c_ref[...] * r
    a_buf[...] = jnp.exp(log_a)
    i_buf[...] = jnp.sqrt(_neg_expm1_2x(log_a)) * (gi * xr)
    ys = [s5_readout(0), s5_readout(1)]
    s5_scan(2)
    s5_scan(3)

    def rg_step(t, hc):
        r0 = pl.multiple_of(t * SUBLANES, SUBLANES)
        hc = a_buf[pl.ds(r0, SUBLANES), :] * hc + i_buf[pl.ds(r0, SUBLANES), :]
        i_buf[pl.ds(r0, SUBLANES), :] = hc
        return hc

    hcar[...] = lax.fori_loop(0, TILE_T, rg_step, hcar[...], unroll=True)
    y_rg = (i_buf[...] * _silu_half(rg_gate)).astype(BF16)
    acc = acc + jnp.dot(y_rg, wo_rg_ref[...], preferred_element_type=F32)

    ys += [s5_readout(2), s5_readout(3)]
    y5 = jnp.concatenate(ys, axis=1) + d_ref[...] * s5_u
    z5 = jax.nn.gelu(y5)
    y5 = z5 * _sigmoid(_dot(z5, wglu_ref[...]) + bglu_ref[...])
    y_s5 = (y5 * _silu_half(s5_gate)).astype(BF16)

    h = acc + jnp.dot(y_s5, wo_s5_ref[...], preferred_element_type=F32)
    gate = _sigmoid(_dot(_rms(h, png_ref[...]), wgate_ref[...]))
    h = h + gate * _dot(_to_time_major(p_ref, p_buf), wproj_ref[...])
    if final:
        _from_time_major(_rms(h, fg_ref[...]), rest[0], o_ref)
    else:
        o_ref[...] = h


N_SC = 8
SC_GC, SC_BRK, SC_RQ, SC_QEG, SC_KDL, SC_RK, SC_BETA, SC_TWC = range(N_SC)
SC_COLS = (SC_GC, SC_BRK, SC_RQ, SC_QEG, SC_KDL)
SC_ROWS = (SC_GC, SC_RK, SC_BETA, SC_TWC)
CONV_RB = 32
GDN_PAR = 4


def _gdn_kernel(h_ref, ng_ref, wqkv_ref, wz_ref, wab_ref, cw_ref, nea_ref, dtb_ref, gn_ref, sel_ref,
                y_ref, *rest, first):
    if first:
        htm_ref, cbuf, hn_buf, qkv_s, sq_s, sc_s, o_s, st_ref, chunk_buf = rest
    else:
        cbuf, hn_buf, qkv_s, sq_s, sc_s, o_s, st_ref = rest

    @pl.when(pl.program_id(0) == 0)
    def _():
        cbuf[0:HALO, :] = jnp.zeros((HALO, 3 * GDN_W), F32)
        st_ref[...] = jnp.zeros_like(st_ref)

    if first:
        h = _to_time_major(h_ref, chunk_buf)
        htm_ref[...] = h
    else:
        h = h_ref[...]
    hn_buf[...] = _rms(h, ng_ref[...]).astype(BF16)

    ab = jnp.dot(hn_buf[...], wab_ref[...], preferred_element_type=F32)
    g = nea_ref[...] * _softplus(ab[:, :LANES] + dtb_ref[...])
    beta = _sigmoid(ab[:, LANES:])
    gc = g
    d = 1
    while d < TILE_T:
        sh = d * SUBLANES
        gc = gc + jnp.concatenate([jnp.zeros((sh, LANES), F32), gc[:ROWS - sh, :]], axis=0)
        d *= 2
    g_last = jnp.broadcast_to(gc[ROWS - SUBLANES:, :][None], (TILE_T, SUBLANES, LANES)).reshape(ROWS, LANES)

    def project(c3):
        cbuf[HALO:HALO + ROWS, c3 * GDN_W:(c3 + 1) * GDN_W] = jnp.dot(
            hn_buf[...], wqkv_ref[:, c3 * GDN_W:(c3 + 1) * GDN_W], preferred_element_type=F32)

    def conv_group(c3):
        for c in range(c3 * GDN_H, (c3 + 1) * GDN_H):
            cols = slice(c * GDN_D, (c + 1) * GDN_D)
            taps = [cw_ref[k:k + 1, cols] for k in range(CONV_K)]
            for r0 in range(0, ROWS, CONV_RB):
                y = taps[CONV_K - 1] * cbuf[r0 + HALO:r0 + HALO + CONV_RB, cols]
                for k in range(CONV_K - 1):
                    off = r0 + HALO - (CONV_K - 1 - k) * SUBLANES
                    y = y + taps[k] * cbuf[off:off + CONV_RB, cols]
                act = _silu_half(y)
                qkv_s[c, r0:r0 + CONV_RB, :] = act
                if c3 < 2:
                    sq_s[r0:r0 + CONV_RB, cols] = (act * act).astype(BF16)

    project(0)
    project(1)
    conv_group(0)
    project(2)
    conv_group(1)
    z = jnp.dot(hn_buf[...], wz_ref[...], preferred_element_type=F32)
    conv_group(2)
    halo = cbuf[ROWS:ROWS + HALO, :]
    cbuf[0:HALO, :] = halo

    ss_q = jnp.dot(sq_s[:, 0:GDN_W], sel_ref[...], preferred_element_type=F32)
    ss_k = jnp.dot(sq_s[:, GDN_W:2 * GDN_W], sel_ref[...], preferred_element_type=F32)
    rq = lax.rsqrt(ss_q + NORM_EPS) * (GDN_D ** -0.5)
    rk = lax.rsqrt(ss_k + NORM_EPS)
    egc = jnp.exp(gc)
    sc_s[SC_GC] = gc
    sc_s[SC_BRK] = beta * rk
    sc_s[SC_RQ] = rq
    sc_s[SC_QEG] = rq * egc
    sc_s[SC_KDL] = rk * jnp.exp(g_last - gc)
    sc_s[SC_RK] = rk
    sc_s[SC_BETA] = beta
    sc_s[SC_TWC] = beta * rk * egc

    row = lax.broadcasted_iota(jnp.int32, (TILE_T, TILE_T), 0)
    col = lax.broadcasted_iota(jnp.int32, (TILE_T, TILE_T), 1)
    causal = row >= col
    strict = row > col
    eye = (row == col).astype(F32)

    def per_group(i, carry):
        chains = [(i * GDN_PAR + s, h) for s in range(GDN_PAR) for h in range(GDN_H)]
        colv, rowv = [], []
        for s in range(GDN_PAR):
            rows = pl.ds(i * GDN_PAR + s, TILE_T, stride=SUBLANES)
            colv.append({n: sc_s[n, rows, :] for n in SC_COLS})
            pad = jnp.zeros((TILE_T, LANES), F32)
            rowv.append({n: jnp.concatenate([sc_s[n, rows, :], pad], axis=0).T for n in SC_ROWS})
        ck = lambda s, n, h: colv[s][n][:, h:h + 1]
        rw = lambda s, n, h: rowv[s][n][h:h + 1, 0:TILE_T]

        q_b, k_b, v_b = [], [], []
        for b, h in chains:
            rows = pl.ds(b, TILE_T, stride=SUBLANES)
            q_b.append(qkv_s[h, rows, :].astype(BF16))
            k_b.append(qkv_s[GDN_H + h, rows, :].astype(BF16))
            v_b.append(qkv_s[2 * GDN_H + h, rows, :].astype(BF16))
        n_ch = len(chains)
        kk = [_dot_nt(jnp.concatenate([k_b[c], q_b[c]], axis=0), k_b[c]) for c in range(n_ch)]
        a_qk, neg = [], []
        for c, (b, h) in enumerate(chains):
            s = c // GDN_H
            diff = ck(s, SC_GC, h) - rw(s, SC_GC, h)
            dk = jnp.where(causal, jnp.exp(jnp.where(causal, diff, 0.0)), 0.0) * rw(s, SC_RK, h)
            neg.append(jnp.where(strict, -(kk[c][:TILE_T] * ck(s, SC_BRK, h)) * dk, 0.0))
            a_qk.append(kk[c][TILE_T:] * ck(s, SC_RQ, h) * dk)
        tinv = [eye + n for n in neg]
        pw = [_dot(n, n) for n in neg]
        span = 2
        while span < TILE_T:
            span *= 2
            if span < TILE_T:
                res = [_dot(jnp.concatenate([t, p], axis=0), p) for t, p in zip(tinv, pw)]
                tinv = [t + r[:TILE_T] for t, r in zip(tinv, res)]
                pw = [r[TILE_T:] for r in res]
            else:
                tinv = [t + _dot(t, p) for t, p in zip(tinv, pw)]
        u, w = [], []
        for c, (b, h) in enumerate(chains):
            s = c // GDN_H
            u.append(_dot(tinv[c] * rw(s, SC_BETA, h), v_b[c]))
            w.append(_dot(tinv[c] * rw(s, SC_TWC, h), k_b[c]))
        state = [st_ref[b * GDN_H + h] for b, h in chains]
        ws_qs = [_dot(jnp.concatenate([w[c].astype(BF16), q_b[c]], axis=0), state[c]) for c in range(n_ch)]
        v_new = [u[c] - ws_qs[c][:TILE_T] for c in range(n_ch)]
        for c, (b, h) in enumerate(chains):
            s = c // GDN_H
            o_s[h, pl.ds(b, TILE_T, stride=SUBLANES), :] = (
                ws_qs[c][TILE_T:] * ck(s, SC_QEG, h) + _dot(a_qk[c], v_new[c]))
        for c, (b, h) in enumerate(chains):
            s = c // GDN_H
            e_last = jnp.exp(rowv[s][SC_GC][h:h + 1, TILE_T - 1:TILE_T])
            st_ref[b * GDN_H + h] = state[c] * e_last + _dot_tn(k_b[c], v_new[c] * ck(s, SC_KDL, h))
        return carry

    lax.fori_loop(0, SUBLANES // GDN_PAR, per_group, 0)

    for h in range(GDN_H):
        sl = slice(h * GDN_D, (h + 1) * GDN_D)
        y_ref[:, sl] = (_rms(o_s[h], gn_ref[...]) * _silu_half(z[:, sl])).astype(y_ref.dtype)


def _layer_const(arr, layer):
    nd = arr.ndim - 1
    return pl.BlockSpec((None,) + arr.shape[1:], lambda i: (layer,) + (0,) * nd, pipeline_mode=pl.Buffered(1))


def _full(shape):
    nd = len(shape)
    return pl.BlockSpec(shape, lambda i: (0,) * nd, pipeline_mode=pl.Buffered(1))


def _rows(width):
    return pl.BlockSpec((ROWS, width), lambda i: (i, 0))


def _seq_block(width):
    return pl.BlockSpec((SUBLANES, TILE_T, width), lambda i: (0, i, 0))


def _seq_params():
    return pltpu.CompilerParams(dimension_semantics=("arbitrary",), vmem_limit_bytes=VMEM_LIMIT)


def _block_diag(blocks):
    n, r, c = blocks.shape[-3:]
    eye = jnp.eye(n, dtype=blocks.dtype)
    return jnp.einsum('...grc,gh->...grhc', blocks, eye).reshape(blocks.shape[:-3] + (n * r, n * c))


def _s5_operators(a_re, a_im, b_re, b_im, c_re, c_im, log_dt):
    depth = a_re.shape[0]
    lam = lax.complex(a_re, a_im)
    dt = jnp.exp(log_dt)[..., None]
    lam_bar = jnp.exp(lam * dt)
    b_bar = ((lam_bar - 1.0) / lam)[..., None] * lax.complex(b_re, b_im)
    nblk = S5_W // LANES
    bt = jnp.swapaxes(b_bar, 2, 3).reshape(depth, nblk, S5_LANE_GROUPS, S5_GROUP, S5_STATE)
    bm = jnp.concatenate([_block_diag(jnp.real(bt)), _block_diag(jnp.imag(bt))], axis=-1)
    ct = jnp.swapaxes(lax.complex(c_re, c_im), 2, 3).reshape(depth, nblk, S5_LANE_GROUPS, S5_STATE, S5_GROUP)
    cm = jnp.concatenate([_block_diag(jnp.real(ct)), -_block_diag(jnp.imag(ct))], axis=-2)
    return (bm.astype(BF16), jnp.real(lam_bar).reshape(depth, 1, -1), jnp.imag(lam_bar).reshape(depth, 1, -1),
            cm.astype(BF16))


def _prepare(norm_g, w_in, rg_conv_w, rg_conv_b, rg_w_a, rg_b_a, rg_w_x, rg_b_x, rg_lambda,
             gdn_conv_w, gdn_a_log, gdn_dt_bias, gdn_norm_g, s5_a_re, s5_a_im, s5_b_re, s5_b_im,
             s5_c_re, s5_c_im, s5_d, s5_log_dt, s5_w_glu, s5_b_glu, w_out, ple_norm_g, ple_w_gate,
             ple_w_proj):
    depth, d_model = norm_g.shape
    o_rgx, o_rgg = 0, RG_W
    o_q = 2 * RG_W
    o_z = o_q + 3 * GDN_W
    o_b = o_z + GDN_W
    o_a = o_b + GDN_H
    o_u = o_a + GDN_H
    o_sg = o_u + S5_W
    cols = lambda c0, width: w_in[:, :, c0:c0 + width]
    row1 = lambda v: v.reshape(depth, 1, -1).astype(F32)
    lane_pad = lambda v: jnp.pad(row1(v), ((0, 0), (0, 0), (0, LANES - GDN_H)))
    zpad = jnp.zeros((depth, d_model, LANES - GDN_H), w_in.dtype)
    nb = RG_W // LANES
    blocks = lambda w: _block_diag(w.reshape(depth, nb, w.shape[1] // nb, w.shape[2], w.shape[3]))
    bm, lam_re, lam_im, cm = _s5_operators(s5_a_re, s5_a_im, s5_b_re, s5_b_im, s5_c_re, s5_c_im, s5_log_dt)
    gdn = dict(
        norm_g=row1(norm_g),
        wqkv=cols(o_q, 3 * GDN_W).astype(BF16),
        wz=(0.5 * cols(o_z, GDN_W)).astype(BF16),
        wab=jnp.concatenate([cols(o_a, GDN_H), zpad, cols(o_b, GDN_H), zpad], axis=2).astype(BF16),
        conv_w=0.5 * gdn_conv_w.astype(F32),
        neg_exp_a=lane_pad(-jnp.exp(gdn_a_log)),
        dt_bias=lane_pad(gdn_dt_bias),
        out_norm_g=row1(gdn_norm_g),
    )
    rg_s5_out = dict(
        norm_g=row1(norm_g),
        w_a=jnp.concatenate([cols(o_rgx, RG_W), 0.5 * cols(o_rgg, RG_W),
                             cols(o_u, S5_W), 0.5 * cols(o_sg, S5_W)], axis=2).astype(BF16),
        conv_w=rg_conv_w.astype(F32),
        conv_b=row1(rg_conv_b),
        wg=jnp.concatenate([blocks(rg_w_a), blocks(rg_w_x)], axis=-1).astype(BF16),
        b_a=row1(rg_b_a),
        b_x=row1(rg_b_x),
        c_rg=row1(-RG_C * jax.nn.softplus(-rg_lambda)),
        bm=bm, lam_re=lam_re, lam_im=lam_im, cm=cm,
        s5_d=row1(s5_d),
        w_glu=s5_w_glu.astype(BF16),
        b_glu=row1(s5_b_glu),
        wo_rg=w_out[:, 0:RG_W].astype(BF16),
        wo_s5=w_out[:, RG_W + GDN_W:].astype(BF16),
        wo_g=w_out[:, RG_W:RG_W + GDN_W].astype(BF16),
        ple_norm_g=row1(ple_norm_g),
        ple_w_gate=ple_w_gate.astype(BF16),
        ple_w_proj=ple_w_proj.astype(BF16),
    )
    return gdn, rg_s5_out


def _layer(h, p, prep, layer, first, final, final_g):
    gdn, rso = prep
    d_model = h.shape[-1]
    n_rows = h.shape[0] * h.shape[1] if first else h.shape[0]
    grid = (n_rows // ROWS,)
    const = lambda arr: _layer_const(arr, layer)

    sel = (jnp.arange(GDN_W)[:, None] // GDN_D == jnp.arange(LANES)[None, :]).astype(BF16)
    gdn_scratch = [pltpu.VMEM((HALO + ROWS, 3 * GDN_W), F32), pltpu.VMEM((ROWS, d_model), BF16),
                   pltpu.VMEM((3 * GDN_H, ROWS, GDN_D), F32),
                   pltpu.VMEM((ROWS, 2 * GDN_W), BF16),
                   pltpu.VMEM((N_SC, ROWS, LANES), F32),
                   pltpu.VMEM((GDN_H, ROWS, GDN_D), F32),
                   pltpu.VMEM((SUBLANES * GDN_H, GDN_D, GDN_D), F32)]
    y_spec, y_shape = _rows(GDN_W), jax.ShapeDtypeStruct((n_rows, GDN_W), BF16)
    if first:
        gdn_scratch.append(pltpu.VMEM((d_model // LANES, ROWS, LANES), F32))
        y_spec, y_shape = [y_spec, _rows(d_model)], [y_shape, jax.ShapeDtypeStruct((n_rows, d_model), F32)]
    gdn_args = [gdn[k] for k in ("norm_g", "wqkv", "wz", "wab", "conv_w", "neg_exp_a", "dt_bias", "out_norm_g")]
    y_g = pl.pallas_call(
        functools.partial(_gdn_kernel, first=first),
        grid=grid,
        in_specs=[_seq_block(d_model) if first else _rows(d_model)] + [const(a) for a in gdn_args]
                 + [_full(sel.shape)],
        out_specs=y_spec,
        out_shape=y_shape,
        scratch_shapes=gdn_scratch,
        compiler_params=_seq_params(),
        name="gdn",
    )(h, *gdn_args, sel)
    if first:
        y_g, h = y_g

    n_state = 2 * S5_BLK_STATE * (S5_W // LANES)
    batch, d_ple = p.shape[1], p.shape[-1]
    scratch = [pltpu.VMEM((HALO + ROWS, RG_W), F32), pltpu.VMEM((ROWS, RG_W), F32),
               pltpu.VMEM((ROWS, RG_W), F32), pltpu.VMEM((SUBLANES, RG_W), F32),
               pltpu.VMEM((ROWS, n_state), F32), pltpu.VMEM((SUBLANES, n_state), F32),
               pltpu.VMEM((d_ple // LANES, ROWS, LANES), F32)]
    if final:
        scratch.append(pltpu.VMEM((d_model // LANES, ROWS, LANES), F32))
        o_spec, o_shape = _seq_block(d_model), jax.ShapeDtypeStruct((batch, n_rows // batch, d_model), F32)
    else:
        o_spec, o_shape = _rows(d_model), jax.ShapeDtypeStruct((n_rows, d_model), F32)
    head = [rso[k] for k in ("norm_g", "w_a", "conv_w", "conv_b", "wg", "b_a", "b_x", "c_rg",
                             "bm", "lam_re", "lam_im", "cm", "s5_d", "w_glu", "b_glu")]
    tail = [rso[k] for k in ("wo_rg", "wo_s5", "wo_g", "ple_norm_g", "ple_w_gate", "ple_w_proj")]
    p_spec = pl.BlockSpec((None, SUBLANES, TILE_T, d_ple), lambda i: (layer, 0, i, 0))
    return pl.pallas_call(
        functools.partial(_rg_s5_out_kernel, final=final),
        grid=grid,
        in_specs=[_rows(d_model)] + [const(a) for a in head] + [_rows(GDN_W), p_spec]
                 + [const(a) for a in tail] + [_full((1, d_model))],
        out_specs=o_spec,
        out_shape=o_shape,
        scratch_shapes=scratch,
        compiler_params=_seq_params(),
        name="rg_s5_out",
    )(h, *head, y_g, p, *tail, final_g.reshape(1, -1).astype(F32))


def kernel(x, p, norm_g, w_in, rg_conv_w, rg_conv_b, rg_w_a, rg_b_a, rg_w_x, rg_b_x, rg_lambda, gdn_conv_w, gdn_a_log, gdn_dt_bias, gdn_norm_g, s5_a_re, s5_a_im, s5_b_re, s5_b_im, s5_c_re, s5_c_im, s5_d, s5_log_dt, s5_w_glu, s5_b_glu, w_out, ple_norm_g, ple_w_gate, ple_w_proj, final_norm_g):
    batch, seq, d_model = x.shape
    depth = p.shape[0]
    assert batch == SUBLANES and seq % TILE_T == 0 and depth >= 1
    assert x.dtype == F32
    prep = _prepare(norm_g, w_in, rg_conv_w, rg_conv_b, rg_w_a, rg_b_a, rg_w_x, rg_b_x, rg_lambda,
                    gdn_conv_w, gdn_a_log, gdn_dt_bias, gdn_norm_g, s5_a_re, s5_a_im, s5_b_re, s5_b_im,
                    s5_c_re, s5_c_im, s5_d, s5_log_dt, s5_w_glu, s5_b_glu, w_out, ple_norm_g, ple_w_gate,
                    ple_w_proj)
    h = x
    for i in range(depth):
        h = _layer(h, p, prep, i, i == 0, i == depth - 1, final_norm_g)
    return h
```
